```python
import jax, jax.numpy as jnp
from jax import lax
import numpy as np

D_MODEL = 2048
BATCH = 2
SEQ = 4096
DEPTH = 1

D_POOL = D_MODEL // 2
N_POOL_GROUPS = 4
POOL_GROUP = D_POOL // N_POOL_GROUPS
POOL_WINDOWS = (2, 4, 8, 16)
D_CONV = D_MODEL // 2
CONV_WIDTH = 31
D_FF = 5632
N_BRANCHES = 2
D_IN = D_POOL + 2 * D_CONV + N_BRANCHES * D_MODEL
EPS = 1e-6

kernel_name = "macaron_gated_pool_conv_encoder"


def rmsnorm(x, g):
    xf = x.astype(jnp.float32)
    y = xf * lax.rsqrt(jnp.mean(xf * xf, axis=-1, keepdims=True) + EPS)
    return (y * g.astype(jnp.float32)).astype(x.dtype)


def layernorm(x, g, b):
    xf = x.astype(jnp.float32)
    mu = jnp.mean(xf, axis=-1, keepdims=True)
    var = jnp.mean(jnp.square(xf - mu), axis=-1, keepdims=True)
    y = (xf - mu) * lax.rsqrt(var + EPS)
    return (y * g.astype(jnp.float32) + b.astype(jnp.float32)).astype(x.dtype)


def swiglu(h, w_gate, w_up, w_down):
    return (jax.nn.silu(h @ w_gate) * (h @ w_up)) @ w_down


def centred_window_mean(u, w):
    T = u.shape[1]
    cs = jnp.cumsum(u.astype(jnp.float32), axis=1)
    cs = jnp.pad(cs, ((0, 0), (1, 0), (0, 0)))
    t = jnp.arange(T)
    lo = jnp.clip(t - w // 2, 0, T)
    hi = jnp.clip(t + (w - w // 2), 0, T)
    count = (hi - lo).astype(jnp.float32)
    s = jnp.take(cs, hi, axis=1) - jnp.take(cs, lo, axis=1)
    return (s / count[None, :, None]).astype(u.dtype)


def pool_branch(u, w_group, scale, w_proj):
    B, T, _ = u.shape
    ug = u.reshape(B, T, N_POOL_GROUPS, POOL_GROUP)
    pooled = jnp.stack(
        [centred_window_mean(ug[:, :, gi], w) - ug[:, :, gi] for gi, w in enumerate(POOL_WINDOWS)],
        axis=2)
    mixed = jnp.einsum('btgc,gcd->btgd', pooled, w_group)
    mixed = mixed.reshape(B, T, D_POOL) * scale
    return mixed @ w_proj


def conv_branch(v, dw_w, dw_b, ln_g, ln_b, w_proj, b_proj):
    v1, v2 = jnp.split(v, 2, axis=-1)
    glu = v1 * jax.nn.sigmoid(v2)
    conv = lax.conv_general_dilated(
        glu, dw_w.reshape(CONV_WIDTH, 1, D_CONV).astype(glu.dtype),
        window_strides=(1,), padding='SAME',
        dimension_numbers=('NWC', 'WIO', 'NWC'),
        feature_group_count=D_CONV) + dw_b
    y = jax.nn.silu(layernorm(conv, ln_g, ln_b))
    return y @ w_proj + b_proj


def setup_inputs(seed: int = 0) -> dict:
    key = jax.random.key(seed)
    ks = iter(jax.random.split(key, 32))
    f32 = jnp.float32

    def nrm(shape, fan_in):
        return jax.random.normal(next(ks), shape, f32) * (fan_in ** -0.5)

    def gain(shape):
        return 1.0 + 0.02 * jax.random.normal(next(ks), shape, f32)

    def bias(shape):
        return 0.02 * jax.random.normal(next(ks), shape, f32)

    L = DEPTH
    return {
        "x": jax.random.normal(next(ks), (BATCH, SEQ, D_MODEL), f32),
        "ffn1_norm": gain((L, D_MODEL)),
        "ffn1_w_gate": nrm((L, D_MODEL, D_FF), D_MODEL),
        "ffn1_w_up": nrm((L, D_MODEL, D_FF), D_MODEL),
        "ffn1_w_down": nrm((L, D_FF, D_MODEL), D_FF),
        "mix_norm": gain((L, D_MODEL)),
        "w_in": nrm((L, D_MODEL, D_IN), D_MODEL),
        "b_gate": bias((L, N_BRANCHES * D_MODEL)),
        "pool_w_group": nrm((L, N_POOL_GROUPS, POOL_GROUP, POOL_GROUP), POOL_GROUP),
        "pool_scale": gain((L, D_POOL)),
        "pool_w_proj": nrm((L, D_POOL, D_MODEL), D_POOL),
        "conv_dw_w": nrm((L, CONV_WIDTH, D_CONV), CONV_WIDTH),
        "conv_dw_b": bias((L, D_CONV)),
        "conv_ln_g": gain((L, D_CONV)),
        "conv_ln_b": bias((L, D_CONV)),
        "conv_w_proj": nrm((L, D_CONV, D_MODEL), D_CONV),
        "conv_b_proj": bias((L, D_MODEL)),
        "w_out": nrm((L, D_MODEL, D_MODEL), D_MODEL),
        "ffn2_norm": gain((L, D_MODEL)),
        "ffn2_w_gate": nrm((L, D_MODEL, D_FF), D_MODEL),
        "ffn2_w_up": nrm((L, D_MODEL, D_FF), D_MODEL),
        "ffn2_w_down": nrm((L, D_FF, D_MODEL), D_FF),
        "final_norm": gain((D_MODEL,)),
    }


def reference(x, ffn1_norm, ffn1_w_gate, ffn1_w_up, ffn1_w_down, mix_norm, w_in, b_gate,
              pool_w_group, pool_scale, pool_w_proj, conv_dw_w, conv_dw_b, conv_ln_g,
              conv_ln_b, conv_w_proj, conv_b_proj, w_out, ffn2_norm, ffn2_w_gate,
              ffn2_w_up, ffn2_w_down, final_norm):
    for l in range(DEPTH):
        x = x + 0.5 * swiglu(rmsnorm(x, ffn1_norm[l]), ffn1_w_gate[l], ffn1_w_up[l], ffn1_w_down[l])
        h = rmsnorm(x, mix_norm[l])
        z = h @ w_in[l]
        u = z[..., :D_POOL]
        v = z[..., D_POOL:D_POOL + 2 * D_CONV]
        g_logits = z[..., D_POOL + 2 * D_CONV:] + b_gate[l]
        a = pool_branch(u, pool_w_group[l], pool_scale[l], pool_w_proj[l])
        b = conv_branch(v, conv_dw_w[l], conv_dw_b[l], conv_ln_g[l], conv_ln_b[l],
                        conv_w_proj[l], conv_b_proj[l])
        g_a, g_b = jnp.split(jax.nn.sigmoid(g_logits), N_BRANCHES, axis=-1)
        x = x + (g_a * a + g_b * b) @ w_out[l]
        x = x + 0.5 * swiglu(rmsnorm(x, ffn2_norm[l]), ffn2_w_gate[l], ffn2_w_up[l], ffn2_w_down[l])
    return rmsnorm(x, final_norm)
```

```python
import functools

import jax
import jax.numpy as jnp
from jax import lax
from jax.experimental import pallas as pl
from jax.experimental.pallas import tpu as pltpu

EPS = 1e-6
POOL_WINDOWS = (2, 4, 8, 16)
HALO = 16
LANES = 128
VMEM_LIMIT = 60 * 1024 * 1024

F32 = jnp.float32
BF16 = jnp.bfloat16


def _rmsnorm(x, g):
    return x * lax.rsqrt(jnp.mean(x * x, axis=-1, keepdims=True) + EPS) * g


def _ffn_kernel(x_ref, g_ref, wg_ref, wu_ref, wd_ref, fn_ref, o_ref, h_ref, *, final_norm):
    f = pl.program_id(1)

    @pl.when(f == 0)
    def _():
        x = x_ref[...]
        h_ref[...] = _rmsnorm(x, g_ref[...]).astype(BF16)
        o_ref[...] = x

    h = h_ref[...]
    gate = jnp.dot(h, wg_ref[...], preferred_element_type=F32)
    up = jnp.dot(h, wu_ref[...], preferred_element_type=F32)
    act = (gate * jax.nn.sigmoid(gate)) * (up * 0.5)
    o_ref[...] += jnp.dot(act.astype(BF16), wd_ref[...], preferred_element_type=F32)

    if final_norm:
        @pl.when(f == pl.num_programs(1) - 1)
        def _():
            o_ref[...] = _rmsnorm(o_ref[...], fn_ref[...])


def _ffn(x, g, wg, wu, wd, fn, *, final_norm, tm=1024, tf=512):
    m, d = x.shape
    dff = wg.shape[1]
    return pl.pallas_call(
        functools.partial(_ffn_kernel, final_norm=final_norm),
        out_shape=jax.ShapeDtypeStruct((m, d), F32),
        grid=(m // tm, dff // tf),
        in_specs=[
            pl.BlockSpec((tm, d), lambda i, f: (i, 0)),
            pl.BlockSpec((1, d), lambda i, f: (0, 0)),
            pl.BlockSpec((d, tf), lambda i, f: (0, f)),
            pl.BlockSpec((d, tf), lambda i, f: (0, f)),
            pl.BlockSpec((tf, d), lambda i, f: (f, 0)),
            pl.BlockSpec((1, d), lambda i, f: (0, 0)),
        ],
        out_specs=pl.BlockSpec((tm, d), lambda i, f: (i, 0)),
        scratch_shapes=[pltpu.VMEM((tm, d), BF16)],
        compiler_params=pltpu.CompilerParams(
            dimension_semantics=("parallel", "arbitrary"),
            vmem_limit_bytes=VMEM_LIMIT),
        name="ffn_final" if final_norm else "ffn",
    )(x, g, wg, wu, wd, fn)


def _in_proj_kernel(x_ref, g_ref, wu_ref, wv1_ref, wv2_ref, u_ref, glu_ref, h_ref):
    @pl.when(pl.program_id(1) == 0)
    def _():
        h_ref[...] = _rmsnorm(x_ref[...], g_ref[...]).astype(BF16)

    h = h_ref[...]
    u_ref[...] = jnp.dot(h, wu_ref[...], preferred_element_type=F32)
    v1 = jnp.dot(h, wv1_ref[...], preferred_element_type=F32)
    v2 = jnp.dot(h, wv2_ref[...], preferred_element_type=F32)
    glu_ref[...] = v1 * jax.nn.sigmoid(v2)


def _in_proj(x, g, wu, wv1, wv2, *, tm=1024, tn=512):
    m, d = x.shape
    c = wu.shape[1]
    w_spec = pl.BlockSpec((d, tn), lambda i, j: (0, j))
    o_spec = pl.BlockSpec((tm, tn), lambda i, j: (i, j))
    return pl.pallas_call(
        _in_proj_kernel,
        out_shape=(jax.ShapeDtypeStruct((m, c), F32), jax.ShapeDtypeStruct((m, c), F32)),
        grid=(m // tm, c // tn),
        in_specs=[
            pl.BlockSpec((tm, d), lambda i, j: (i, 0)),
            pl.BlockSpec((1, d), lambda i, j: (0, 0)),
            w_spec, w_spec, w_spec,
        ],
        out_specs=(o_spec, o_spec),
        scratch_shapes=[pltpu.VMEM((tm, d), BF16)],
        compiler_params=pltpu.CompilerParams(
            dimension_semantics=("parallel", "arbitrary"),
            vmem_limit_bytes=VMEM_LIMIT),
        name="in_proj",
    )(x, g, wu, wv1, wv2)


def _fill_halo_buffer(buf_ref, prev_ref, main_ref, next_ref, at_seq_start, at_seq_end, tm):
    prev = prev_ref[...]
    nxt = next_ref[...]
    buf_ref[0:HALO, :] = jnp.where(at_seq_start, jnp.zeros_like(prev), prev)
    buf_ref[HALO:HALO + tm, :] = main_ref[...]
    buf_ref[HALO + tm:HALO + tm + HALO, :] = jnp.where(at_seq_end, jnp.zeros_like(nxt), nxt)


def _mixer_kernel(x_ref, up_ref, um_ref, un_ref, gp_ref, gm_ref, gn_ref,
                  mixg_ref, wgate_ref, bgate_ref, wgrp_ref, pscale_ref, pproj_ref,
                  dww_ref, dwb_ref, lng_ref, lnb_ref, cproj_ref, cbias_ref, wout_ref,
                  o_ref, ubuf_ref, gbuf_ref, conv_ref, *, tm, seq_len, conv_width, row_chunk):
    i = pl.program_id(0)
    blocks_per_seq = seq_len // tm
    seq_block = i % blocks_per_seq
    at_seq_start = seq_block == 0
    at_seq_end = seq_block == blocks_per_seq - 1
    d_pool = um_ref.shape[1]
    d_conv = gm_ref.shape[1]
    d_model = x_ref.shape[1]
    group = d_pool // len(POOL_WINDOWS)

    _fill_halo_buffer(ubuf_ref, up_ref, um_ref, un_ref, at_seq_start, at_seq_end, tm)
    _fill_halo_buffer(gbuf_ref, gp_ref, gm_ref, gn_ref, at_seq_start, at_seq_end, tm)

    pos = seq_block * tm + lax.broadcasted_iota(jnp.int32, (tm, 1), 0)
    mixed = []
    for gi, w in enumerate(POOL_WINDOWS):
        cols = slice(gi * group, (gi + 1) * group)
        lo = jnp.maximum(pos - w // 2, 0)
        hi = jnp.minimum(pos + (w - w // 2), seq_len)
        inv_count = 1.0 / (hi - lo).astype(F32)
        s = ubuf_ref[HALO - w // 2:HALO - w // 2 + tm, cols]
        for dlt in range(-(w // 2) + 1, w - w // 2):
            s = s + ubuf_ref[HALO + dlt:HALO + dlt + tm, cols]
        pooled = s * inv_count - ubuf_ref[HALO:HALO + tm, cols]
        mg = jnp.dot(pooled.astype(BF16), wgrp_ref[gi], preferred_element_type=F32)
        mixed.append((mg * pscale_ref[:, cols]).astype(BF16))
    mixed = jnp.concatenate(mixed, axis=1)
    a = jnp.dot(mixed, pproj_ref[...], preferred_element_type=F32)

    radius = conv_width // 2

    def conv_lane_chunk(c, carry):
        lanes = pl.ds(pl.multiple_of(c * LANES, LANES), LANES)
        w = dww_ref[:, lanes]
        bias = dwb_ref[:, lanes]
        for r in range(tm // row_chunk):
            base = HALO - radius + r * row_chunk
            acc = gbuf_ref[base:base + row_chunk, lanes] * w[0:1, :]
            for k in range(1, conv_width):
                acc = acc + gbuf_ref[base + k:base + k + row_chunk, lanes] * w[k:k + 1, :]
            conv_ref[r * row_chunk:(r + 1) * row_chunk, lanes] = acc + bias
        return carry

    lax.fori_loop(0, d_conv // LANES, conv_lane_chunk, 0)

    conv = conv_ref[...]
    mu = jnp.mean(conv, axis=-1, keepdims=True)
    cen = conv - mu
    var = jnp.mean(cen * cen, axis=-1, keepdims=True)
    ln = cen * lax.rsqrt(var + EPS) * lng_ref[...] + lnb_ref[...]
    y = (ln * jax.nn.sigmoid(ln)).astype(BF16)
    b = jnp.dot(y, cproj_ref[...], preferred_element_type=F32) + cbias_ref[...]

    x = x_ref[...]
    h = _rmsnorm(x, mixg_ref[...]).astype(BF16)
    gates = jax.nn.sigmoid(jnp.dot(h, wgate_ref[...], preferred_element_type=F32) + bgate_ref[...])
    mix = gates[:, :d_model] * a + gates[:, d_model:] * b
    o_ref[...] = x + jnp.dot(mix.astype(BF16), wout_ref[...], preferred_element_type=F32)


def _mixer(x, u, glu, mix_norm, w_gate, b_gate, w_group, pool_scale, pool_proj,
           dw_w, dw_b, ln_g, ln_b, conv_proj, conv_bias, w_out, *, seq_len, tm=256, row_chunk=64):
    m, d = x.shape
    d_pool = u.shape[1]
    d_conv = glu.shape[1]
    conv_width = dw_w.shape[0]
    assert seq_len % tm == 0 and tm % HALO == 0 and conv_width // 2 < HALO
    hb = tm // HALO
    last_hb = m // HALO - 1

    def resident(shape):
        return pl.BlockSpec(shape, lambda i: (0,) * len(shape), pipeline_mode=pl.Buffered(1))

    def halo_specs(c):
        return [
            pl.BlockSpec((HALO, c), lambda i: (jnp.maximum(i * hb - 1, 0), 0)),
            pl.BlockSpec((tm, c), lambda i: (i, 0)),
            pl.BlockSpec((HALO, c), lambda i: (jnp.minimum((i + 1) * hb, last_hb), 0)),
        ]

    kern = functools.partial(_mixer_kernel, tm=tm, seq_len=seq_len,
                             conv_width=conv_width, row_chunk=row_chunk)
    return pl.pallas_call(
        kern,
        out_shape=jax.ShapeDtypeStruct((m, d), F32),
        grid=(m // tm,),
        in_specs=[pl.BlockSpec((tm, d), lambda i: (i, 0))]
        + halo_specs(d_pool) + halo_specs(d_conv)
        + [resident(a.shape) for a in (mix_norm, w_gate, b_gate, w_group, pool_scale, pool_proj,
                                       dw_w, dw_b, ln_g, ln_b, conv_proj, conv_bias, w_out)],
        out_specs=pl.BlockSpec((tm, d), lambda i: (i, 0)),
        scratch_shapes=[
            pltpu.VMEM((tm + 2 * HALO, d_pool), F32),
            pltpu.VMEM((tm + 2 * HALO, d_conv), F32),
            pltpu.VMEM((tm, d_conv), F32),
        ],
        compiler_params=pltpu.CompilerParams(
            dimension_semantics=("parallel",),
            vmem_limit_bytes=VMEM_LIMIT),
        name="mixer",
    )(x, u, u, u, glu, glu, glu, mix_norm, w_gate, b_gate, w_group, pool_scale, pool_proj,
      dw_w, dw_b, ln_g, ln_b, conv_proj, conv_bias, w_out)


def kernel(x, ffn1_norm, ffn1_w_gate, ffn1_w_up, ffn1_w_down, mix_norm, w_in, b_gate, pool_w_group, pool_scale, pool_w_proj, conv_dw_w, conv_dw_b, conv_ln_g, conv_ln_b, conv_w_proj, conv_b_proj, w_out, ffn2_norm, ffn2_w_gate, ffn2_w_up, ffn2_w_down, final_norm):
    bsz, seq_len, d = x.shape
    depth = ffn1_norm.shape[0]
    d_pool = pool_scale.shape[1]
    d_conv = conv_dw_b.shape[1]
    row = lambda v: v.reshape(1, -1)
    fn = row(final_norm)

    xf = x.reshape(bsz * seq_len, d)
    for l in range(depth):
        xf = _ffn(xf, row(ffn1_norm[l]), ffn1_w_gate[l].astype(BF16), ffn1_w_up[l].astype(BF16),
                  ffn1_w_down[l].astype(BF16), fn, final_norm=False)
        w_in_l = w_in[l].astype(BF16)
        w_u = w_in_l[:, :d_pool]
        w_v1 = w_in_l[:, d_pool:d_pool + d_conv]
        w_v2 = w_in_l[:, d_pool + d_conv:d_pool + 2 * d_conv]
        w_g = w_in_l[:, d_pool + 2 * d_conv:]
        u, glu = _in_proj(xf, row(mix_norm[l]), w_u, w_v1, w_v2)
        xf = _mixer(xf, u, glu, row(mix_norm[l]), w_g, row(b_gate[l]),
                    pool_w_group[l].astype(BF16), row(pool_scale[l]), pool_w_proj[l].astype(BF16),
                    conv_dw_w[l], row(conv_dw_b[l]), row(conv_ln_g[l]), row(conv_ln_b[l]),
                    conv_w_proj[l].astype(BF16), row(conv_b_proj[l]), w_out[l].astype(BF16),
                    seq_len=seq_len)
        xf = _ffn(xf, row(ffn2_norm[l]), ffn2_w_gate[l].astype(BF16), ffn2_w_up[l].astype(BF16),
                  ffn2_w_down[l].astype(BF16), fn, final_norm=(l == depth - 1))
    return xf.reshape(bsz, seq_len, d)
```

```python
import functools

import jax
import jax.numpy as jnp
from jax import lax
from jax.experimental import pallas as pl
from jax.experimental.pallas import tpu as pltpu

EPS = 1e-6
POOL_WINDOWS = (2, 4, 8, 16)
HALO = 16
LANES = 128
VMEM_LIMIT = 60 * 1024 * 1024

F32 = jnp.float32
BF16 = jnp.bfloat16


def _rmsnorm(x, g):
    return x * lax.rsqrt(jnp.mean(x * x, axis=-1, keepdims=True) + EPS) * g


def _ffn_kernel(x_ref, g_ref, wg_ref, wu_ref, wd_ref, fn_ref, o_ref, h_ref, *, final_norm):
    f = pl.program_id(1)

    @pl.when(f == 0)
    def _():
        x = x_ref[...]
        h_ref[...] = _rmsnorm(x, g_ref[...]).astype(BF16)
        o_ref[...] = x

    h = h_ref[...]
    gate = jnp.dot(h, wg_ref[...], preferred_element_type=F32)
    up = jnp.dot(h, wu_ref[...], preferred_element_type=F32)
    act = (gate * jax.nn.sigmoid(gate)) * (up * 0.5)
    o_ref[...] += jnp.dot(act.astype(BF16), wd_ref[...], preferred_element_type=F32)

    if final_norm:
        @pl.when(f == pl.num_programs(1) - 1)
        def _():
            o_ref[...] = _rmsnorm(o_ref[...], fn_ref[...])


def _ffn(x, g, wg, wu, wd, fn, *, final_norm, tm=1024, tf=512):
    m, d = x.shape
    dff = wg.shape[1]
    return pl.pallas_call(
        functools.partial(_ffn_kernel, final_norm=final_norm),
        out_shape=jax.ShapeDtypeStruct((m, d), F32),
        grid=(m // tm, dff // tf),
        in_specs=[
            pl.BlockSpec((tm, d), lambda i, f: (i, 0)),
            pl.BlockSpec((1, d), lambda i, f: (0, 0)),
            pl.BlockSpec((d, tf), lambda i, f: (0, f)),
            pl.BlockSpec((d, tf), lambda i, f: (0, f)),
            pl.BlockSpec((tf, d), lambda i, f: (f, 0)),
            pl.BlockSpec((1, d), lambda i, f: (0, 0)),
        ],
        out_specs=pl.BlockSpec((tm, d), lambda i, f: (i, 0)),
        scratch_shapes=[pltpu.VMEM((tm, d), BF16)],
        compiler_params=pltpu.CompilerParams(
            dimension_semantics=("parallel", "arbitrary"),
            vmem_limit_bytes=VMEM_LIMIT),
        name="ffn_final" if final_norm else "ffn",
    )(x, g, wg, wu, wd, fn)


def _in_proj_kernel(x_ref, g_ref, wu_ref, wv1_ref, wv2_ref, u_ref, glu_ref, h_ref):
    @pl.when(pl.program_id(1) == 0)
    def _():
        h_ref[...] = _rmsnorm(x_ref[...], g_ref[...]).astype(BF16)

    h = h_ref[...]
    u_ref[...] = jnp.dot(h, wu_ref[...], preferred_element_type=F32)
    v1 = jnp.dot(h, wv1_ref[...], preferred_element_type=F32)
    v2 = jnp.dot(h, wv2_ref[...], preferred_element_type=F32)
    glu_ref[...] = v1 * jax.nn.sigmoid(v2)


def _in_proj(x, g, wu, wv1, wv2, *, tm=1024, tn=512):
    m, d = x.shape
    c = wu.shape[1]
    w_spec = pl.BlockSpec((d, tn), lambda i, j: (0, j))
    o_spec = pl.BlockSpec((tm, tn), lambda i, j: (i, j))
    return pl.pallas_call(
        _in_proj_kernel,
        out_shape=(jax.ShapeDtypeStruct((m, c), F32), jax.ShapeDtypeStruct((m, c), F32)),
        grid=(m // tm, c // tn),
        in_specs=[
            pl.BlockSpec((tm, d), lambda i, j: (i, 0)),
            pl.BlockSpec((1, d), lambda i, j: (0, 0)),
            w_spec, w_spec, w_spec,
        ],
        out_specs=(o_spec, o_spec),
        scratch_shapes=[pltpu.VMEM((tm, d), BF16)],
        compiler_params=pltpu.CompilerParams(
            dimension_semantics=("parallel", "arbitrary"),
            vmem_limit_bytes=VMEM_LIMIT),
        name="in_proj",
    )(x, g, wu, wv1, wv2)


def _fill_halo_buffer(buf_ref, prev_ref, main_ref, next_ref, at_seq_start, at_seq_end, tm):
    for c in range(buf_ref.shape[0]):
        lanes = slice(c * LANES, (c + 1) * LANES)
        prev = prev_ref[:, lanes]
        nxt = next_ref[:, lanes]
        buf_ref[c, 0:HALO, :] = jnp.where(at_seq_start, jnp.zeros_like(prev), prev)
        buf_ref[c, HALO:HALO + tm, :] = main_ref[:, lanes]
        buf_ref[c, HALO + tm:HALO + tm + HALO, :] = jnp.where(at_seq_end, jnp.zeros_like(nxt), nxt)


def _mixer_kernel(x_ref, up_ref, um_ref, un_ref, gp_ref, gm_ref, gn_ref,
                  mixg_ref, wgate_ref, bgate_ref, wgrp_ref, pscale_ref, pproj_ref,
                  dww_ref, dwb_ref, lng_ref, lnb_ref, cproj_ref, cbias_ref, wout_ref,
                  o_ref, ubuf_ref, gbuf_ref, conv_ref, h_ref, gates_ref,
                  *, tm, seq_len, conv_width, row_chunk):
    i = pl.program_id(0)
    blocks_per_seq = seq_len // tm
    seq_block = i % blocks_per_seq
    at_seq_start = seq_block == 0
    at_seq_end = seq_block == blocks_per_seq - 1
    d_conv = gm_ref.shape[1]
    d_model = x_ref.shape[1]
    group_chunks = um_ref.shape[1] // len(POOL_WINDOWS) // LANES

    _fill_halo_buffer(ubuf_ref, up_ref, um_ref, un_ref, at_seq_start, at_seq_end, tm)
    _fill_halo_buffer(gbuf_ref, gp_ref, gm_ref, gn_ref, at_seq_start, at_seq_end, tm)
    x = x_ref[...]
    h_ref[...] = _rmsnorm(x, mixg_ref[...]).astype(BF16)

    first = HALO - conv_width // 2
    n_chunks = d_conv // LANES
    gate_cols = gates_ref.shape[1] // n_chunks
    for c in range(n_chunks):
        lanes = slice(c * LANES, (c + 1) * LANES)
        w_rows = [jnp.broadcast_to(dww_ref[k:k + 1, lanes], (row_chunk, LANES))
                  for k in range(conv_width)]
        bias = dwb_ref[:, lanes]
        for t0 in range(0, tm, row_chunk):
            acc = gbuf_ref[c, t0 + first:t0 + first + row_chunk, :] * w_rows[0]
            for k in range(1, conv_width):
                acc = acc + gbuf_ref[c, t0 + first + k:t0 + first + k + row_chunk, :] * w_rows[k]
            conv_ref[t0:t0 + row_chunk, lanes] = acc + bias

        gc = slice(c * gate_cols, (c + 1) * gate_cols)
        logits = jnp.dot(h_ref[...], wgate_ref[:, gc], preferred_element_type=F32) + bgate_ref[:, gc]
        gates_ref[:, gc] = jax.nn.sigmoid(logits)

    pos = seq_block * tm + lax.broadcasted_iota(jnp.int32, (tm, 1), 0)
    mixed = []
    for gi, w in enumerate(POOL_WINDOWS):
        lo = jnp.maximum(pos - w // 2, 0)
        hi = jnp.minimum(pos + (w - w // 2), seq_len)
        inv_count = 1.0 / (hi - lo).astype(F32)
        pooled = []
        for c in range(gi * group_chunks, (gi + 1) * group_chunks):
            s = ubuf_ref[c, HALO - w // 2:HALO - w // 2 + tm, :]
            for dlt in range(-(w // 2) + 1, w - w // 2):
                s = s + ubuf_ref[c, HALO + dlt:HALO + dlt + tm, :]
            pooled.append((s * inv_count - ubuf_ref[c, HALO:HALO + tm, :]).astype(BF16))
        mg = jnp.dot(jnp.concatenate(pooled, axis=1), wgrp_ref[gi], preferred_element_type=F32)
        cols = slice(gi * group_chunks * LANES, (gi + 1) * group_chunks * LANES)
        mixed.append((mg * pscale_ref[:, cols]).astype(BF16))
    mixed = jnp.concatenate(mixed, axis=1)
    a = jnp.dot(mixed, pproj_ref[...], preferred_element_type=F32)

    conv = conv_ref[...]
    mu = jnp.mean(conv, axis=-1, keepdims=True)
    cen = conv - mu
    var = jnp.mean(cen * cen, axis=-1, keepdims=True)
    ln = cen * lax.rsqrt(var + EPS) * lng_ref[...] + lnb_ref[...]
    y = (ln * jax.nn.sigmoid(ln)).astype(BF16)
    b = jnp.dot(y, cproj_ref[...], preferred_element_type=F32) + cbias_ref[...]

    mix = gates_ref[:, :d_model] * a + gates_ref[:, d_model:] * b
    o_ref[...] = x + jnp.dot(mix.astype(BF16), wout_ref[...], preferred_element_type=F32)


def _mixer(x, u, glu, mix_norm, w_gate, b_gate, w_group, pool_scale, pool_proj,
           dw_w, dw_b, ln_g, ln_b, conv_proj, conv_bias, w_out, *, seq_len, tm=256, row_chunk=128):
    m, d = x.shape
    d_pool = u.shape[1]
    d_conv = glu.shape[1]
    conv_width = dw_w.shape[0]
    assert seq_len % tm == 0 and tm % HALO == 0 and conv_width // 2 < HALO
    hb = tm // HALO
    last_hb = m // HALO - 1

    def resident(shape):
        return pl.BlockSpec(shape, lambda i: (0,) * len(shape), pipeline_mode=pl.Buffered(1))

    def halo_specs(c):
        return [
            pl.BlockSpec((HALO, c), lambda i: (jnp.maximum(i * hb - 1, 0), 0)),
            pl.BlockSpec((tm, c), lambda i: (i, 0)),
            pl.BlockSpec((HALO, c), lambda i: (jnp.minimum((i + 1) * hb, last_hb), 0)),
        ]

    kern = functools.partial(_mixer_kernel, tm=tm, seq_len=seq_len,
                             conv_width=conv_width, row_chunk=row_chunk)
    return pl.pallas_call(
        kern,
        out_shape=jax.ShapeDtypeStruct((m, d), F32),
        grid=(m // tm,),
        in_specs=[pl.BlockSpec((tm, d), lambda i: (i, 0))]
        + halo_specs(d_pool) + halo_specs(d_conv)
        + [resident(a.shape) for a in (mix_norm, w_gate, b_gate, w_group, pool_scale, pool_proj,
                                       dw_w, dw_b, ln_g, ln_b, conv_proj, conv_bias, w_out)],
        out_specs=pl.BlockSpec((tm, d), lambda i: (i, 0)),
        scratch_shapes=[
            pltpu.VMEM((d_pool // LANES, tm + 2 * HALO, LANES), F32),
            pltpu.VMEM((d_conv // LANES, tm + 2 * HALO, LANES), F32),
            pltpu.VMEM((tm, d_conv), F32),
            pltpu.VMEM((tm, d), BF16),
            pltpu.VMEM((tm, w_gate.shape[1]), F32),
        ],
        compiler_params=pltpu.CompilerParams(
            dimension_semantics=("parallel",),
            vmem_limit_bytes=VMEM_LIMIT),
        name="mixer",
    )(x, u, u, u, glu, glu, glu, mix_norm, w_gate, b_gate, w_group, pool_scale, pool_proj,
      dw_w, dw_b, ln_g, ln_b, conv_proj, conv_bias, w_out)


def kernel(x, ffn1_norm, ffn1_w_gate, ffn1_w_up, ffn1_w_down, mix_norm, w_in, b_gate, pool_w_group, pool_scale, pool_w_proj, conv_dw_w, conv_dw_b, conv_ln_g, conv_ln_b, conv_w_proj, conv_b_proj, w_out, ffn2_norm, ffn2_w_gate, ffn2_w_up, ffn2_w_down, final_norm):
    bsz, seq_len, d = x.shape
    depth = ffn1_norm.shape[0]
    d_pool = pool_scale.shape[1]
    d_conv = conv_dw_b.shape[1]
    row = lambda v: v.reshape(1, -1)
    fn = row(final_norm)

    xf = x.reshape(bsz * seq_len, d)
    for l in range(depth):
        xf = _ffn(xf, row(ffn1_norm[l]), ffn1_w_gate[l].astype(BF16), ffn1_w_up[l].astype(BF16),
                  ffn1_w_down[l].astype(BF16), fn, final_norm=False)
        w_in_l = w_in[l].astype(BF16)
        w_u = w_in_l[:, :d_pool]
        w_v1 = w_in_l[:, d_pool:d_pool + d_conv]
        w_v2 = w_in_l[:, d_pool + d_conv:d_pool + 2 * d_conv]
        w_g = w_in_l[:, d_pool + 2 * d_conv:]
        u, glu = _in_proj(xf, row(mix_norm[l]), w_u, w_v1, w_v2)
        xf = _mixer(xf, u, glu, row(mix_norm[l]), w_g, row(b_gate[l]),
                    pool_w_group[l].astype(BF16), row(pool_scale[l]), pool_w_proj[l].astype(BF16),
                    conv_dw_w[l], row(conv_dw_b[l]), row(conv_ln_g[l]), row(conv_ln_b[l]),
                    conv_w_proj[l].astype(BF16), row(conv_b_proj[l]), w_out[l].astype(BF16),
                    seq_len=seq_len)
        xf = _ffn(xf, row(ffn2_norm[l]), ffn2_w_gate[l].astype(BF16), ffn2_w_up[l].astype(BF16),
                  ffn2_w_down[l].astype(BF16), fn, final_norm=(l == depth - 1))
    return xf.reshape(bsz, seq_len, d)
```

```python
import functools

import jax
import jax.numpy as jnp
from jax import lax
from jax.experimental import pallas as pl
from jax.experimental.pallas import tpu as pltpu

EPS = 1e-6
POOL_WINDOWS = (2, 4, 8, 16)
HALO = 16
LANES = 128
BF16_ROWS = 16
VMEM_LIMIT = 60 * 1024 * 1024

F32 = jnp.float32
BF16 = jnp.bfloat16


def _rmsnorm(x, g):
    return x * lax.rsqrt(jnp.mean(x * x, axis=-1, keepdims=True) + EPS) * g


def _cast_specs(job, grid):
    src, c0, cn = job
    gi, gf = grid
    rows = src.shape[0]
    br = rows // gi
    assert rows % gi == 0 and br % BF16_ROWS == 0
    nf = max(n for n in range(1, gf + 1)
             if cn % n == 0 and (cn // n) % LANES == 0 and c0 % (cn // n) == 0)
    bc = cn // nf
    off = c0 // bc
    in_spec = pl.BlockSpec((br, bc), lambda i, f: (i, off + jnp.minimum(f, nf - 1)))
    out_spec = pl.BlockSpec((br, bc), lambda i, f: (i, jnp.minimum(f, nf - 1)))
    return in_spec, out_spec, jax.ShapeDtypeStruct((rows, cn), BF16)


def _run_casts(src_refs, dst_refs):
    for src_ref, dst_ref in zip(src_refs, dst_refs):
        dst_ref[...] = src_ref[...].astype(BF16)


def _ffn_kernel(x_ref, g_ref, wg_ref, wu_ref, wd_ref, fn_ref, *rest, final_norm, n_casts):
    cast_srcs, o_ref = rest[:n_casts], rest[n_casts]
    cast_dsts, h_ref = rest[n_casts + 1:2 * n_casts + 1], rest[2 * n_casts + 1]
    f = pl.program_id(1)

    @pl.when(f == 0)
    def _():
        x = x_ref[...]
        h_ref[...] = _rmsnorm(x, g_ref[...]).astype(BF16)
        o_ref[...] = x

    _run_casts(cast_srcs, cast_dsts)
    h = h_ref[...]
    gate = jnp.dot(h, wg_ref[...], preferred_element_type=F32)
    up = jnp.dot(h, wu_ref[...], preferred_element_type=F32)
    act = (gate * jax.nn.sigmoid(gate)) * (up * 0.5)
    o_ref[...] += jnp.dot(act.astype(BF16), wd_ref[...], preferred_element_type=F32)

    if final_norm:
        @pl.when(f == pl.num_programs(1) - 1)
        def _():
            o_ref[...] = _rmsnorm(o_ref[...], fn_ref[...])


def _ffn(x, g, wg, wu, wd, fn, *, final_norm, casts=(), tm=1024, tf=512):
    m, d = x.shape
    dff = wg.shape[1]
    grid = (m // tm, dff // tf)
    cast_in, cast_out, cast_shapes = zip(*[_cast_specs(job, grid) for job in casts]) if casts else ((), (), ())
    out, *cast_results = pl.pallas_call(
        functools.partial(_ffn_kernel, final_norm=final_norm, n_casts=len(casts)),
        out_shape=(jax.ShapeDtypeStruct((m, d), F32), *cast_shapes),
        grid=grid,
        in_specs=[
            pl.BlockSpec((tm, d), lambda i, f: (i, 0)),
            pl.BlockSpec((1, d), lambda i, f: (0, 0)),
            pl.BlockSpec((d, tf), lambda i, f: (0, f)),
            pl.BlockSpec((d, tf), lambda i, f: (0, f)),
            pl.BlockSpec((tf, d), lambda i, f: (f, 0)),
            pl.BlockSpec((1, d), lambda i, f: (0, 0)),
            *cast_in,
        ],
        out_specs=(pl.BlockSpec((tm, d), lambda i, f: (i, 0)), *cast_out),
        scratch_shapes=[pltpu.VMEM((tm, d), BF16)],
        compiler_params=pltpu.CompilerParams(
            dimension_semantics=("parallel", "arbitrary"),
            vmem_limit_bytes=VMEM_LIMIT),
        name="ffn_final" if final_norm else "ffn",
    )(x, g, wg, wu, wd, fn, *[job[0] for job in casts])
    return out, cast_results


def _in_proj_kernel(x_ref, g_ref, wu_ref, wv1_ref, wv2_ref, *rest, n_casts):
    cast_srcs, (u_ref, glu_ref) = rest[:n_casts], rest[n_casts:n_casts + 2]
    cast_dsts, h_ref = rest[n_casts + 2:2 * n_casts + 2], rest[2 * n_casts + 2]
    @pl.when(pl.program_id(1) == 0)
    def _():
        h_ref[...] = _rmsnorm(x_ref[...], g_ref[...]).astype(BF16)

    _run_casts(cast_srcs, cast_dsts)
    h = h_ref[...]
    u_ref[...] = jnp.dot(h, wu_ref[...], preferred_element_type=F32)
    v1 = jnp.dot(h, wv1_ref[...], preferred_element_type=F32)
    v2 = jnp.dot(h, wv2_ref[...], preferred_element_type=F32)
    glu_ref[...] = v1 * jax.nn.sigmoid(v2)


def _in_proj(x, g, w_uv, *, c, casts=(), tm=1024, tn=256):
    m, d = x.shape
    assert w_uv.shape[1] == 3 * c and c % tn == 0
    nb = c // tn
    grid = (m // tm, nb)
    cast_in, cast_out, cast_shapes = zip(*[_cast_specs(job, grid) for job in casts]) if casts else ((), (), ())
    o_spec = pl.BlockSpec((tm, tn), lambda i, j: (i, j))
    u, glu, *cast_results = pl.pallas_call(
        functools.partial(_in_proj_kernel, n_casts=len(casts)),
        out_shape=(jax.ShapeDtypeStruct((m, c), F32), jax.ShapeDtypeStruct((m, c), F32), *cast_shapes),
        grid=grid,
        in_specs=[
            pl.BlockSpec((tm, d), lambda i, j: (i, 0)),
            pl.BlockSpec((1, d), lambda i, j: (0, 0)),
            pl.BlockSpec((d, tn), lambda i, j: (0, j)),
            pl.BlockSpec((d, tn), lambda i, j: (0, nb + j)),
            pl.BlockSpec((d, tn), lambda i, j: (0, 2 * nb + j)),
            *cast_in,
        ],
        out_specs=(o_spec, o_spec, *cast_out),
        scratch_shapes=[pltpu.VMEM((tm, d), BF16)],
        compiler_params=pltpu.CompilerParams(
            dimension_semantics=("parallel", "arbitrary"),
            vmem_limit_bytes=VMEM_LIMIT),
        name="in_proj",
    )(x, g, w_uv, w_uv, w_uv, *[job[0] for job in casts])
    return u, glu, cast_results


def _fill_halo_buffer(buf_ref, prev_ref, main_ref, next_ref, at_seq_start, at_seq_end, tm):
    for c in range(buf_ref.shape[0]):
        lanes = slice(c * LANES, (c + 1) * LANES)
        prev = prev_ref[:, lanes]
        nxt = next_ref[:, lanes]
        buf_ref[c, 0:HALO, :] = jnp.where(at_seq_start, jnp.zeros_like(prev), prev)
        buf_ref[c, HALO:HALO + tm, :] = main_ref[:, lanes]
        buf_ref[c, HALO + tm:HALO + tm + HALO, :] = jnp.where(at_seq_end, jnp.zeros_like(nxt), nxt)


def _mixer_kernel(x_ref, up_ref, um_ref, un_ref, gp_ref, gm_ref, gn_ref,
                  mixg_ref, wgate_ref, bgate_ref, wgrp_ref, pscale_ref, pproj_ref,
                  dww_ref, dwb_ref, lng_ref, lnb_ref, cproj_ref, cbias_ref, wout_ref,
                  o_ref, ubuf_ref, gbuf_ref, conv_ref, h_ref, gates_ref,
                  *, tm, seq_len, conv_width, row_chunk):
    i = pl.program_id(0)
    blocks_per_seq = seq_len // tm
    seq_block = i % blocks_per_seq
    at_seq_start = seq_block == 0
    at_seq_end = seq_block == blocks_per_seq - 1
    d_conv = gm_ref.shape[1]
    d_model = x_ref.shape[1]
    group_chunks = um_ref.shape[1] // len(POOL_WINDOWS) // LANES

    _fill_halo_buffer(ubuf_ref, up_ref, um_ref, un_ref, at_seq_start, at_seq_end, tm)
    _fill_halo_buffer(gbuf_ref, gp_ref, gm_ref, gn_ref, at_seq_start, at_seq_end, tm)
    x = x_ref[...]
    h_ref[...] = _rmsnorm(x, mixg_ref[...]).astype(BF16)

    first = HALO - conv_width // 2
    n_chunks = d_conv // LANES
    gate_cols = gates_ref.shape[1] // n_chunks
    for c in range(n_chunks):
        lanes = slice(c * LANES, (c + 1) * LANES)
        w_rows = [jnp.broadcast_to(dww_ref[k:k + 1, lanes], (row_chunk, LANES))
                  for k in range(conv_width)]
        bias = dwb_ref[:, lanes]
        for t0 in range(0, tm, row_chunk):
            acc = gbuf_ref[c, t0 + first:t0 + first + row_chunk, :] * w_rows[0]
            for k in range(1, conv_width):
                acc = acc + gbuf_ref[c, t0 + first + k:t0 + first + k + row_chunk, :] * w_rows[k]
            conv_ref[t0:t0 + row_chunk, lanes] = acc + bias

        gc = slice(c * gate_cols, (c + 1) * gate_cols)
        logits = jnp.dot(h_ref[...], wgate_ref[:, gc], preferred_element_type=F32) + bgate_ref[:, gc]
        gates_ref[:, gc] = jax.nn.sigmoid(logits)

    pos = seq_block * tm + lax.broadcasted_iota(jnp.int32, (tm, 1), 0)
    mixed = []
    for gi, w in enumerate(POOL_WINDOWS):
        lo = jnp.maximum(pos - w // 2, 0)
        hi = jnp.minimum(pos + (w - w // 2), seq_len)
        inv_count = 1.0 / (hi - lo).astype(F32)
        pooled = []
        for c in range(gi * group_chunks, (gi + 1) * group_chunks):
            s = ubuf_ref[c, HALO - w // 2:HALO - w // 2 + tm, :]
            for dlt in range(-(w // 2) + 1, w - w // 2):
                s = s + ubuf_ref[c, HALO + dlt:HALO + dlt + tm, :]
            pooled.append((s * inv_count - ubuf_ref[c, HALO:HALO + tm, :]).astype(BF16))
        mg = jnp.dot(jnp.concatenate(pooled, axis=1), wgrp_ref[gi], preferred_element_type=F32)
        cols = slice(gi * group_chunks * LANES, (gi + 1) * group_chunks * LANES)
        mixed.append((mg * pscale_ref[:, cols]).astype(BF16))
    mixed = jnp.concatenate(mixed, axis=1)
    a = jnp.dot(mixed, pproj_ref[...], preferred_element_type=F32)

    conv = conv_ref[...]
    mu = jnp.mean(conv, axis=-1, keepdims=True)
    cen = conv - mu
    var = jnp.mean(cen * cen, axis=-1, keepdims=True)
    ln = cen * lax.rsqrt(var + EPS) * lng_ref[...] + lnb_ref[...]
    y = (ln * jax.nn.sigmoid(ln)).astype(BF16)
    b = jnp.dot(y, cproj_ref[...], preferred_element_type=F32) + cbias_ref[...]

    mix = gates_ref[:, :d_model] * a + gates_ref[:, d_model:] * b
    o_ref[...] = x + jnp.dot(mix.astype(BF16), wout_ref[...], preferred_element_type=F32)


def _mixer(x, u, glu, mix_norm, w_gate, b_gate, w_group, pool_scale, pool_proj,
           dw_w, dw_b, ln_g, ln_b, conv_proj, conv_bias, w_out, *, seq_len, tm=256, row_chunk=128):
    m, d = x.shape
    d_pool = u.shape[1]
    d_conv = glu.shape[1]
    conv_width = dw_w.shape[0]
    assert seq_len % tm == 0 and tm % HALO == 0 and conv_width // 2 < HALO
    hb = tm // HALO
    last_hb = m // HALO - 1

    def resident(shape):
        return pl.BlockSpec(shape, lambda i: (0,) * len(shape), pipeline_mode=pl.Buffered(1))

    def halo_specs(c):
        return [
            pl.BlockSpec((HALO, c), lambda i: (jnp.maximum(i * hb - 1, 0), 0)),
            pl.BlockSpec((tm, c), lambda i: (i, 0)),
            pl.BlockSpec((HALO, c), lambda i: (jnp.minimum((i + 1) * hb, last_hb), 0)),
        ]

    kern = functools.partial(_mixer_kernel, tm=tm, seq_len=seq_len,
                             conv_width=conv_width, row_chunk=row_chunk)
    return pl.pallas_call(
        kern,
        out_shape=jax.ShapeDtypeStruct((m, d), F32),
        grid=(m // tm,),
        in_specs=[pl.BlockSpec((tm, d), lambda i: (i, 0))]
        + halo_specs(d_pool) + halo_specs(d_conv)
        + [resident(a.shape) for a in (mix_norm, w_gate, b_gate, w_group, pool_scale, pool_proj,
                                       dw_w, dw_b, ln_g, ln_b, conv_proj, conv_bias, w_out)],
        out_specs=pl.BlockSpec((tm, d), lambda i: (i, 0)),
        scratch_shapes=[
            pltpu.VMEM((d_pool // LANES, tm + 2 * HALO, LANES), F32),
            pltpu.VMEM((d_conv // LANES, tm + 2 * HALO, LANES), F32),
            pltpu.VMEM((tm, d_conv), F32),
            pltpu.VMEM((tm, d), BF16),
            pltpu.VMEM((tm, w_gate.shape[1]), F32),
        ],
        compiler_params=pltpu.CompilerParams(
            dimension_semantics=("parallel",),
            vmem_limit_bytes=VMEM_LIMIT),
        name="mixer",
    )(x, u, u, u, glu, glu, glu, mix_norm, w_gate, b_gate, w_group, pool_scale, pool_proj,
      dw_w, dw_b, ln_g, ln_b, conv_proj, conv_bias, w_out)


def kernel(x, ffn1_norm, ffn1_w_gate, ffn1_w_up, ffn1_w_down, mix_norm, w_in, b_gate, pool_w_group, pool_scale, pool_w_proj, conv_dw_w, conv_dw_b, conv_ln_g, conv_ln_b, conv_w_proj, conv_b_proj, w_out, ffn2_norm, ffn2_w_gate, ffn2_w_up, ffn2_w_down, final_norm):
    bsz, seq_len, d = x.shape
    depth = ffn1_norm.shape[0]
    d_pool = pool_scale.shape[1]
    d_conv = conv_dw_b.shape[1]
    row = lambda v: v.reshape(1, -1)
    fn = row(final_norm)

    xf = x.reshape(bsz * seq_len, d)
    for l in range(depth):
        assert d_pool == d_conv
        n_uv = d_pool + 2 * d_conv
        whole = lambda w: (w, 0, w.shape[1])
        xf, (w_uv, w_g, wg2, wu2) = _ffn(
            xf, row(ffn1_norm[l]), ffn1_w_gate[l].astype(BF16), ffn1_w_up[l].astype(BF16),
            ffn1_w_down[l].astype(BF16), fn, final_norm=False,
            casts=[(w_in[l], 0, n_uv), (w_in[l], n_uv, w_in.shape[2] - n_uv),
                   whole(ffn2_w_gate[l]), whole(ffn2_w_up[l])])
        n_grp, grp = pool_w_group.shape[1:3]
        u, glu, (wd2, w_o, w_pp, w_cp, w_grp) = _in_proj(
            xf, row(mix_norm[l]), w_uv, c=d_pool,
            casts=[whole(ffn2_w_down[l]), whole(w_out[l]), whole(pool_w_proj[l]),
                   whole(conv_w_proj[l]), whole(pool_w_group[l].reshape(n_grp * grp, grp))])
        xf = _mixer(xf, u, glu, row(mix_norm[l]), w_g, row(b_gate[l]),
                    w_grp.reshape(n_grp, grp, grp), row(pool_scale[l]), w_pp,
                    conv_dw_w[l], row(conv_dw_b[l]), row(conv_ln_g[l]), row(conv_ln_b[l]),
                    w_cp, row(conv_b_proj[l]), w_o, seq_len=seq_len)
        xf, _ = _ffn(xf, row(ffn2_norm[l]), wg2, wu2, wd2, fn, final_norm=(l == depth - 1))
    return xf.reshape(bsz, seq_len, d)
```

```python
import functools

import jax
import jax.numpy as jnp
from jax import lax
from jax.experimental import pallas as pl
from jax.experimental.pallas import tpu as pltpu

EPS = 1e-6
POOL_WINDOWS = (2, 4, 8, 16)
HALO = 16
LANES = 128
BF16_ROWS = 16
VMEM_LIMIT = 60 * 1024 * 1024

F32 = jnp.float32
BF16 = jnp.bfloat16


def _rmsnorm(x, g):
    return x * lax.rsqrt(jnp.mean(x * x, axis=-1, keepdims=True) + EPS) * g


def _cast_specs(job, grid):
    src, c0, cn = job
    gi, gf = grid
    rows = src.shape[0]
    br = rows // gi
    assert rows % gi == 0 and br % BF16_ROWS == 0
    nf = max(n for n in range(1, gf + 1)
             if cn % n == 0 and (cn // n) % LANES == 0 and c0 % (cn // n) == 0)
    bc = cn // nf
    off = c0 // bc
    in_spec = pl.BlockSpec((br, bc), lambda i, f: (i, off + jnp.minimum(f, nf - 1)))
    out_spec = pl.BlockSpec((br, bc), lambda i, f: (i, jnp.minimum(f, nf - 1)))
    return in_spec, out_spec, jax.ShapeDtypeStruct((rows, cn), BF16)


def _run_casts(src_refs, dst_refs):
    for src_ref, dst_ref in zip(src_refs, dst_refs):
        dst_ref[...] = src_ref[...].astype(BF16)


def _ffn_kernel(x_ref, g_ref, wg_ref, wu_ref, wd_ref, fn_ref, *rest, final_norm, n_casts):
    cast_srcs, o_ref = rest[:n_casts], rest[n_casts]
    cast_dsts, h_ref = rest[n_casts + 1:2 * n_casts + 1], rest[2 * n_casts + 1]
    f = pl.program_id(1)

    @pl.when(f == 0)
    def _():
        x = x_ref[...]
        h_ref[...] = _rmsnorm(x, g_ref[...]).astype(BF16)
        o_ref[...] = x

    _run_casts(cast_srcs, cast_dsts)
    h = h_ref[...]
    gate = jnp.dot(h, wg_ref[...], preferred_element_type=F32)
    up = jnp.dot(h, wu_ref[...], preferred_element_type=F32)
    act = (gate * jax.nn.sigmoid(gate)) * (up * 0.5)
    o_ref[...] += jnp.dot(act.astype(BF16), wd_ref[...], preferred_element_type=F32)

    if final_norm:
        @pl.when(f == pl.num_programs(1) - 1)
        def _():
            o_ref[...] = _rmsnorm(o_ref[...], fn_ref[...])


def _ffn(x, g, wg, wu, wd, fn, *, final_norm, casts=(), tm=1024, tf=512):
    m, d = x.shape
    dff = wg.shape[1]
    grid = (m // tm, dff // tf)
    cast_in, cast_out, cast_shapes = zip(*[_cast_specs(job, grid) for job in casts]) if casts else ((), (), ())
    out, *cast_results = pl.pallas_call(
        functools.partial(_ffn_kernel, final_norm=final_norm, n_casts=len(casts)),
        out_shape=(jax.ShapeDtypeStruct((m, d), F32), *cast_shapes),
        grid=grid,
        in_specs=[
            pl.BlockSpec((tm, d), lambda i, f: (i, 0)),
            pl.BlockSpec((1, d), lambda i, f: (0, 0)),
            pl.BlockSpec((d, tf), lambda i, f: (0, f)),
            pl.BlockSpec((d, tf), lambda i, f: (0, f)),
            pl.BlockSpec((tf, d), lambda i, f: (f, 0)),
            pl.BlockSpec((1, d), lambda i, f: (0, 0)),
            *cast_in,
        ],
        out_specs=(pl.BlockSpec((tm, d), lambda i, f: (i, 0)), *cast_out),
        scratch_shapes=[pltpu.VMEM((tm, d), BF16)],
        compiler_params=pltpu.CompilerParams(
            dimension_semantics=("parallel", "arbitrary"),
            vmem_limit_bytes=VMEM_LIMIT),
        name="ffn_final" if final_norm else "ffn",
    )(x, g, wg, wu, wd, fn, *[job[0] for job in casts])
    return out, cast_results


def _in_proj_kernel(x_ref, g_ref, wu_ref, wv1_ref, wv2_ref, *rest, n_casts):
    cast_srcs, (u_ref, glu_ref, h_ref) = rest[:n_casts], rest[n_casts:n_casts + 3]
    cast_dsts = rest[n_casts + 3:]
    @pl.when(pl.program_id(1) == 0)
    def _():
        h_ref[...] = _rmsnorm(x_ref[...], g_ref[...]).astype(BF16)

    _run_casts(cast_srcs, cast_dsts)
    h = h_ref[...]
    u_ref[...] = jnp.dot(h, wu_ref[...], preferred_element_type=F32)
    v1 = jnp.dot(h, wv1_ref[...], preferred_element_type=F32)
    v2 = jnp.dot(h, wv2_ref[...], preferred_element_type=F32)
    glu_ref[...] = v1 * jax.nn.sigmoid(v2)


def _in_proj(x, g, w_uv, *, c, casts=(), tm=1024, tn=256):
    m, d = x.shape
    assert w_uv.shape[1] == 3 * c and c % tn == 0
    nb = c // tn
    grid = (m // tm, nb)
    cast_in, cast_out, cast_shapes = zip(*[_cast_specs(job, grid) for job in casts]) if casts else ((), (), ())
    o_spec = pl.BlockSpec((tm, tn), lambda i, j: (i, j))
    u, glu, h, *cast_results = pl.pallas_call(
        functools.partial(_in_proj_kernel, n_casts=len(casts)),
        out_shape=(jax.ShapeDtypeStruct((m, c), F32), jax.ShapeDtypeStruct((m, c), F32),
                   jax.ShapeDtypeStruct((m, d), BF16), *cast_shapes),
        grid=grid,
        in_specs=[
            pl.BlockSpec((tm, d), lambda i, j: (i, 0)),
            pl.BlockSpec((1, d), lambda i, j: (0, 0)),
            pl.BlockSpec((d, tn), lambda i, j: (0, j)),
            pl.BlockSpec((d, tn), lambda i, j: (0, nb + j)),
            pl.BlockSpec((d, tn), lambda i, j: (0, 2 * nb + j)),
            *cast_in,
        ],
        out_specs=(o_spec, o_spec, pl.BlockSpec((tm, d), lambda i, j: (i, 0)), *cast_out),
        compiler_params=pltpu.CompilerParams(
            dimension_semantics=("parallel", "arbitrary"),
            vmem_limit_bytes=VMEM_LIMIT),
        name="in_proj",
    )(x, g, w_uv, w_uv, w_uv, *[job[0] for job in casts])
    return u, glu, h, cast_results


def _zero_bits(v):
    bits = lax.bitcast_convert_type(v, jnp.uint32)
    return ((bits >> 16) >> 16)[0:1, :]


def _tied(w, zero_bits):
    if zero_bits is None:
        return w
    return lax.bitcast_convert_type(lax.bitcast_convert_type(w, jnp.uint32) | zero_bits, F32)


def _fill_halo_buffer(buf_ref, prev_ref, main_ref, next_ref, at_seq_start, at_seq_end, tm):
    for c in range(buf_ref.shape[0]):
        lanes = slice(c * LANES, (c + 1) * LANES)
        prev = prev_ref[:, lanes]
        nxt = next_ref[:, lanes]
        buf_ref[c, 0:HALO, :] = jnp.where(at_seq_start, jnp.zeros_like(prev), prev)
        buf_ref[c, HALO:HALO + tm, :] = main_ref[:, lanes]
        buf_ref[c, HALO + tm:HALO + tm + HALO, :] = jnp.where(at_seq_end, jnp.zeros_like(nxt), nxt)


def _mixer_kernel(x_ref, h_ref, up_ref, um_ref, un_ref, gp_ref, gm_ref, gn_ref,
                  wgate_ref, bgate_ref, wgrp_ref, pscale_ref, pproj_ref,
                  dww_ref, dwb_ref, lng_ref, lnb_ref, cproj_ref, cbias_ref, wout_ref,
                  o_ref, ubuf_ref, gbuf_ref, conv_ref, gates_ref,
                  *, tm, seq_len, conv_width, row_chunk):
    i = pl.program_id(0)
    blocks_per_seq = seq_len // tm
    seq_block = i % blocks_per_seq
    at_seq_start = seq_block == 0
    at_seq_end = seq_block == blocks_per_seq - 1
    d_conv = gm_ref.shape[1]
    d_model = x_ref.shape[1]
    group_chunks = um_ref.shape[1] // len(POOL_WINDOWS) // LANES

    _fill_halo_buffer(ubuf_ref, up_ref, um_ref, un_ref, at_seq_start, at_seq_end, tm)
    _fill_halo_buffer(gbuf_ref, gp_ref, gm_ref, gn_ref, at_seq_start, at_seq_end, tm)
    x = x_ref[...]

    first = HALO - conv_width // 2
    n_chunks = d_conv // LANES
    gate_cols = gates_ref.shape[1] // n_chunks
    tie = None
    for c in range(n_chunks):
        lanes = slice(c * LANES, (c + 1) * LANES)
        w_rows = [jnp.broadcast_to(_tied(dww_ref[k:k + 1, lanes], tie), (row_chunk, LANES))
                  for k in range(conv_width)]
        bias = dwb_ref[:, lanes]
        for t0 in range(0, tm, row_chunk):
            acc = gbuf_ref[c, t0 + first:t0 + first + row_chunk, :] * w_rows[0]
            for k in range(1, conv_width):
                acc = acc + gbuf_ref[c, t0 + first + k:t0 + first + k + row_chunk, :] * w_rows[k]
            conv_ref[t0:t0 + row_chunk, lanes] = acc + bias

        gc = slice(c * gate_cols, (c + 1) * gate_cols)
        logits = jnp.dot(h_ref[...], wgate_ref[:, gc], preferred_element_type=F32) + bgate_ref[:, gc]
        sig = jax.nn.sigmoid(logits)
        gates_ref[:, gc] = sig
        tie = _zero_bits(sig[tm - 8:tm, gate_cols - LANES:gate_cols])

    pos = seq_block * tm + lax.broadcasted_iota(jnp.int32, (tm, 1), 0)
    mixed = []
    for gi, w in enumerate(POOL_WINDOWS):
        lo = jnp.maximum(pos - w // 2, 0)
        hi = jnp.minimum(pos + (w - w // 2), seq_len)
        inv_count = 1.0 / (hi - lo).astype(F32)
        pooled = []
        for c in range(gi * group_chunks, (gi + 1) * group_chunks):
            s = ubuf_ref[c, HALO - w // 2:HALO - w // 2 + tm, :]
            for dlt in range(-(w // 2) + 1, w - w // 2):
                s = s + ubuf_ref[c, HALO + dlt:HALO + dlt + tm, :]
            pooled.append((s * inv_count - ubuf_ref[c, HALO:HALO + tm, :]).astype(BF16))
        mg = jnp.dot(jnp.concatenate(pooled, axis=1), wgrp_ref[gi], preferred_element_type=F32)
        cols = slice(gi * group_chunks * LANES, (gi + 1) * group_chunks * LANES)
        mixed.append((mg * pscale_ref[:, cols]).astype(BF16))
    mixed = jnp.concatenate(mixed, axis=1)
    a = jnp.dot(mixed, pproj_ref[...], preferred_element_type=F32)

    conv = conv_ref[...]
    mu = jnp.mean(conv, axis=-1, keepdims=True)
    cen = conv - mu
    var = jnp.mean(cen * cen, axis=-1, keepdims=True)
    ln = cen * lax.rsqrt(var + EPS) * lng_ref[...] + lnb_ref[...]
    y = (ln * jax.nn.sigmoid(ln)).astype(BF16)
    b = jnp.dot(y, cproj_ref[...], preferred_element_type=F32) + cbias_ref[...]

    mix = gates_ref[:, :d_model] * a + gates_ref[:, d_model:] * b
    o_ref[...] = x + jnp.dot(mix.astype(BF16), wout_ref[...], preferred_element_type=F32)


def _mixer(x, h, u, glu, w_gate, b_gate, w_group, pool_scale, pool_proj,
           dw_w, dw_b, ln_g, ln_b, conv_proj, conv_bias, w_out, *, seq_len, tm=256, row_chunk=128):
    m, d = x.shape
    d_pool = u.shape[1]
    d_conv = glu.shape[1]
    conv_width = dw_w.shape[0]
    assert seq_len % tm == 0 and tm % HALO == 0 and conv_width // 2 < HALO
    hb = tm // HALO
    last_hb = m // HALO - 1

    def resident(shape):
        return pl.BlockSpec(shape, lambda i: (0,) * len(shape), pipeline_mode=pl.Buffered(1))

    def halo_specs(c):
        return [
            pl.BlockSpec((HALO, c), lambda i: (jnp.maximum(i * hb - 1, 0), 0)),
            pl.BlockSpec((tm, c), lambda i: (i, 0)),
            pl.BlockSpec((HALO, c), lambda i: (jnp.minimum((i + 1) * hb, last_hb), 0)),
        ]

    kern = functools.partial(_mixer_kernel, tm=tm, seq_len=seq_len,
                             conv_width=conv_width, row_chunk=row_chunk)
    return pl.pallas_call(
        kern,
        out_shape=jax.ShapeDtypeStruct((m, d), F32),
        grid=(m // tm,),
        in_specs=[pl.BlockSpec((tm, d), lambda i: (i, 0)), pl.BlockSpec((tm, d), lambda i: (i, 0))]
        + halo_specs(d_pool) + halo_specs(d_conv)
        + [resident(a.shape) for a in (w_gate, b_gate, w_group, pool_scale, pool_proj,
                                       dw_w, dw_b, ln_g, ln_b, conv_proj, conv_bias, w_out)],
        out_specs=pl.BlockSpec((tm, d), lambda i: (i, 0)),
        scratch_shapes=[
            pltpu.VMEM((d_pool // LANES, tm + 2 * HALO, LANES), F32),
            pltpu.VMEM((d_conv // LANES, tm + 2 * HALO, LANES), F32),
            pltpu.VMEM((tm, d_conv), F32),
            pltpu.VMEM((tm, w_gate.shape[1]), F32),
        ],
        compiler_params=pltpu.CompilerParams(
            dimension_semantics=("parallel",),
            vmem_limit_bytes=VMEM_LIMIT),
        name="mixer",
    )(x, h, u, u, u, glu, glu, glu, w_gate, b_gate, w_group, pool_scale, pool_proj,
      dw_w, dw_b, ln_g, ln_b, conv_proj, conv_bias, w_out)


def kernel(x, ffn1_norm, ffn1_w_gate, ffn1_w_up, ffn1_w_down, mix_norm, w_in, b_gate, pool_w_group, pool_scale, pool_w_proj, conv_dw_w, conv_dw_b, conv_ln_g, conv_ln_b, conv_w_proj, conv_b_proj, w_out, ffn2_norm, ffn2_w_gate, ffn2_w_up, ffn2_w_down, final_norm):
    bsz, seq_len, d = x.shape
    depth = ffn1_norm.shape[0]
    d_pool = pool_scale.shape[1]
    d_conv = conv_dw_b.shape[1]
    row = lambda v: v.reshape(1, -1)
    fn = row(final_norm)

    xf = x.reshape(bsz * seq_len, d)
    for l in range(depth):
        assert d_pool == d_conv
        n_uv = d_pool + 2 * d_conv
        whole = lambda w: (w, 0, w.shape[1])
        xf, (w_uv, w_g, wg2, wu2) = _ffn(
            xf, row(ffn1_norm[l]), ffn1_w_gate[l].astype(BF16), ffn1_w_up[l].astype(BF16),
            ffn1_w_down[l].astype(BF16), fn, final_norm=False,
            casts=[(w_in[l], 0, n_uv), (w_in[l], n_uv, w_in.shape[2] - n_uv),
                   whole(ffn2_w_gate[l]), whole(ffn2_w_up[l])])
        n_grp, grp = pool_w_group.shape[1:3]
        u, glu, h, (wd2, w_o, w_pp, w_cp, w_grp) = _in_proj(
            xf, row(mix_norm[l]), w_uv, c=d_pool,
            casts=[whole(ffn2_w_down[l]), whole(w_out[l]), whole(pool_w_proj[l]),
                   whole(conv_w_proj[l]), whole(pool_w_group[l].reshape(n_grp * grp, grp))])
        xf = _mixer(xf, h, u, glu, w_g, row(b_gate[l]),
                    w_grp.reshape(n_grp, grp, grp), row(pool_scale[l]), w_pp,
                    conv_dw_w[l], row(conv_dw_b[l]), row(conv_ln_g[l]), row(conv_ln_b[l]),
                    w_cp, row(conv_b_proj[l]), w_o, seq_len=seq_len)
        xf, _ = _ffn(xf, row(ffn2_norm[l]), wg2, wu2, wd2, fn, final_norm=(l == depth - 1))
    return xf.reshape(bsz, seq_len, d)
```

```python
import functools

import jax
import jax.numpy as jnp
from jax import lax
from jax.experimental import pallas as pl
from jax.experimental.pallas import tpu as pltpu

EPS = 1e-6
POOL_WINDOWS = (2, 4, 8, 16)
HALO = 16
LANES = 128
BF16_ROWS = 16
VMEM_LIMIT = 62 * 1024 * 1024

F32 = jnp.float32
BF16 = jnp.bfloat16


def _rmsnorm(x, g):
    return x * lax.rsqrt(jnp.mean(x * x, axis=-1, keepdims=True) + EPS) * g


def _cast_specs(job, grid):
    src, c0, cn = job
    gi, gf = grid if len(grid) == 2 else (grid[0], 1)
    rows = src.shape[0]
    br = rows // gi
    assert rows % gi == 0 and br % BF16_ROWS == 0
    nf = max(n for n in range(1, gf + 1)
             if cn % n == 0 and (cn // n) % LANES == 0 and c0 % (cn // n) == 0)
    bc = cn // nf
    off = c0 // bc
    col = lambda g: jnp.minimum(g[1], nf - 1) if len(g) == 2 else 0
    in_spec = pl.BlockSpec((br, bc), lambda *g: (g[0], off + col(g)))
    out_spec = pl.BlockSpec((br, bc), lambda *g: (g[0], col(g)))
    return in_spec, out_spec, jax.ShapeDtypeStruct((rows, cn), BF16)


def _plan_casts(casts, grid):
    return zip(*[_cast_specs(job, grid) for job in casts]) if casts else ((), (), ())


def _run_casts(src_refs, dst_refs):
    for src_ref, dst_ref in zip(src_refs, dst_refs):
        dst_ref[...] = src_ref[...].astype(BF16)


def _ffn_kernel(x_ref, g_ref, wg_ref, wu_ref, wd_ref, fn_ref, *rest, final_norm, n_casts):
    cast_srcs, o_ref = rest[:n_casts], rest[n_casts]
    cast_dsts, h_ref = rest[n_casts + 1:2 * n_casts + 1], rest[2 * n_casts + 1]
    f = pl.program_id(1)

    @pl.when(f == 0)
    def _():
        x = x_ref[...]
        h_ref[...] = _rmsnorm(x, g_ref[...]).astype(BF16)
        o_ref[...] = x

    _run_casts(cast_srcs, cast_dsts)
    h = h_ref[...]
    gate = jnp.dot(h, wg_ref[...], preferred_element_type=F32)
    up = jnp.dot(h, wu_ref[...], preferred_element_type=F32)
    act = (gate * jax.nn.sigmoid(gate)) * (up * 0.5)
    o_ref[...] += jnp.dot(act.astype(BF16), wd_ref[...], preferred_element_type=F32)

    if final_norm:
        @pl.when(f == pl.num_programs(1) - 1)
        def _():
            o_ref[...] = _rmsnorm(o_ref[...], fn_ref[...])


def _ffn(x, g, wg, wu, wd, fn, *, final_norm, casts=(), tm=1024, tf=512):
    m, d = x.shape
    dff = wg.shape[1]
    grid = (m // tm, dff // tf)
    cast_in, cast_out, cast_shapes = _plan_casts(casts, grid)
    out, *cast_results = pl.pallas_call(
        functools.partial(_ffn_kernel, final_norm=final_norm, n_casts=len(casts)),
        out_shape=(jax.ShapeDtypeStruct((m, d), F32), *cast_shapes),
        grid=grid,
        in_specs=[
            pl.BlockSpec((tm, d), lambda i, f: (i, 0)),
            pl.BlockSpec((1, d), lambda i, f: (0, 0)),
            pl.BlockSpec((d, tf), lambda i, f: (0, f)),
            pl.BlockSpec((d, tf), lambda i, f: (0, f)),
            pl.BlockSpec((tf, d), lambda i, f: (f, 0)),
            pl.BlockSpec((1, d), lambda i, f: (0, 0)),
            *cast_in,
        ],
        out_specs=(pl.BlockSpec((tm, d), lambda i, f: (i, 0)), *cast_out),
        scratch_shapes=[pltpu.VMEM((tm, d), BF16)],
        compiler_params=pltpu.CompilerParams(
            dimension_semantics=("parallel", "arbitrary"),
            vmem_limit_bytes=VMEM_LIMIT),
        name="ffn_final" if final_norm else "ffn",
    )(x, g, wg, wu, wd, fn, *[job[0] for job in casts])
    return out, cast_results


def _in_proj_kernel(x_ref, g_ref, wu_ref, wv1_ref, wv2_ref, u_ref, glu_ref, h_ref):
    @pl.when(pl.program_id(1) == 0)
    def _():
        h_ref[...] = _rmsnorm(x_ref[...], g_ref[...]).astype(BF16)

    h = h_ref[...]
    u_ref[...] = jnp.dot(h, wu_ref[...], preferred_element_type=F32)
    v1 = jnp.dot(h, wv1_ref[...], preferred_element_type=F32)
    v2 = jnp.dot(h, wv2_ref[...], preferred_element_type=F32)
    glu_ref[...] = v1 * jax.nn.sigmoid(v2)


def _in_proj(x, g, w_uv, *, c, tm=1024, tn=512):
    m, d = x.shape
    assert w_uv.shape[1] == 3 * c and c % tn == 0
    nb = c // tn
    o_spec = pl.BlockSpec((tm, tn), lambda i, j: (i, j))
    return pl.pallas_call(
        _in_proj_kernel,
        out_shape=(jax.ShapeDtypeStruct((m, c), F32), jax.ShapeDtypeStruct((m, c), F32),
                   jax.ShapeDtypeStruct((m, d), BF16)),
        grid=(m // tm, nb),
        in_specs=[
            pl.BlockSpec((tm, d), lambda i, j: (i, 0)),
            pl.BlockSpec((1, d), lambda i, j: (0, 0)),
            pl.BlockSpec((d, tn), lambda i, j: (0, j)),
            pl.BlockSpec((d, tn), lambda i, j: (0, nb + j)),
            pl.BlockSpec((d, tn), lambda i, j: (0, 2 * nb + j)),
        ],
        out_specs=(o_spec, o_spec, pl.BlockSpec((tm, d), lambda i, j: (i, 0))),
        compiler_params=pltpu.CompilerParams(
            dimension_semantics=("parallel", "arbitrary"),
            vmem_limit_bytes=VMEM_LIMIT),
        name="in_proj",
    )(x, g, w_uv, w_uv, w_uv)


def _zero_bits(v):
    bits = lax.bitcast_convert_type(v, jnp.uint32)
    return ((bits >> 16) >> 16)[0:1, :]


def _tied(w, zero_bits):
    if zero_bits is None:
        return w
    return lax.bitcast_convert_type(lax.bitcast_convert_type(w, jnp.uint32) | zero_bits, F32)


def _fill_halo_buffer(buf_ref, prev_ref, main_ref, next_ref, at_seq_start, at_seq_end, tm):
    for c in range(buf_ref.shape[0]):
        lanes = slice(c * LANES, (c + 1) * LANES)
        prev = prev_ref[:, lanes]
        nxt = next_ref[:, lanes]
        buf_ref[c, 0:HALO, :] = jnp.where(at_seq_start, jnp.zeros_like(prev), prev)
        buf_ref[c, HALO:HALO + tm, :] = main_ref[:, lanes]
        buf_ref[c, HALO + tm:HALO + tm + HALO, :] = jnp.where(at_seq_end, jnp.zeros_like(nxt), nxt)


def _mixer_kernel(x_ref, h_ref, up_ref, um_ref, un_ref, gp_ref, gm_ref, gn_ref,
                  wgate_ref, bgate_ref, wgrp_ref, pscale_ref, pproj_ref,
                  dww_ref, dwb_ref, lng_ref, lnb_ref, cproj_ref, cbias_ref, wout_ref, *rest,
                  tm, seq_len, conv_width, row_chunk, n_casts):
    cast_srcs, o_ref = rest[:n_casts], rest[n_casts]
    cast_dsts = rest[n_casts + 1:2 * n_casts + 1]
    ubuf_ref, gbuf_ref, conv_ref, gates_ref = rest[2 * n_casts + 1:]
    i = pl.program_id(0)
    blocks_per_seq = seq_len // tm
    seq_block = i % blocks_per_seq
    at_seq_start = seq_block == 0
    at_seq_end = seq_block == blocks_per_seq - 1
    d_conv = gm_ref.shape[1]
    d_model = x_ref.shape[1]
    group_chunks = um_ref.shape[1] // len(POOL_WINDOWS) // LANES

    _run_casts(cast_srcs, cast_dsts)
    _fill_halo_buffer(ubuf_ref, up_ref, um_ref, un_ref, at_seq_start, at_seq_end, tm)
    _fill_halo_buffer(gbuf_ref, gp_ref, gm_ref, gn_ref, at_seq_start, at_seq_end, tm)
    x = x_ref[...]

    first = HALO - conv_width // 2
    n_chunks = d_conv // LANES
    gate_cols = gates_ref.shape[1] // n_chunks
    tie = None
    for c in range(n_chunks):
        lanes = slice(c * LANES, (c + 1) * LANES)
        w_rows = [jnp.broadcast_to(_tied(dww_ref[k:k + 1, lanes], tie), (row_chunk, LANES))
                  for k in range(conv_width)]
        bias = dwb_ref[:, lanes]
        for t0 in range(0, tm, row_chunk):
            acc = gbuf_ref[c, t0 + first:t0 + first + row_chunk, :] * w_rows[0]
            for k in range(1, conv_width):
                acc = acc + gbuf_ref[c, t0 + first + k:t0 + first + k + row_chunk, :] * w_rows[k]
            conv_ref[t0:t0 + row_chunk, lanes] = acc + bias

        gc = slice(c * gate_cols, (c + 1) * gate_cols)
        logits = jnp.dot(h_ref[...], wgate_ref[:, gc], preferred_element_type=F32) + bgate_ref[:, gc]
        sig = jax.nn.sigmoid(logits)
        gates_ref[:, gc] = sig
        tie = _zero_bits(sig[tm - 8:tm, gate_cols - LANES:gate_cols])

    pos = seq_block * tm + lax.broadcasted_iota(jnp.int32, (tm, 1), 0)
    mixed = []
    for gi, w in enumerate(POOL_WINDOWS):
        lo = jnp.maximum(pos - w // 2, 0)
        hi = jnp.minimum(pos + (w - w // 2), seq_len)
        inv_count = 1.0 / (hi - lo).astype(F32)
        pooled = []
        for c in range(gi * group_chunks, (gi + 1) * group_chunks):
            s = ubuf_ref[c, HALO - w // 2:HALO - w // 2 + tm, :]
            for dlt in range(-(w // 2) + 1, w - w // 2):
                s = s + ubuf_ref[c, HALO + dlt:HALO + dlt + tm, :]
            pooled.append((s * inv_count - ubuf_ref[c, HALO:HALO + tm, :]).astype(BF16))
        mg = jnp.dot(jnp.concatenate(pooled, axis=1), wgrp_ref[gi], preferred_element_type=F32)
        cols = slice(gi * group_chunks * LANES, (gi + 1) * group_chunks * LANES)
        mixed.append((mg * pscale_ref[:, cols]).astype(BF16))
    mixed = jnp.concatenate(mixed, axis=1)
    a = jnp.dot(mixed, pproj_ref[...], preferred_element_type=F32)

    conv = conv_ref[...]
    mu = jnp.mean(conv, axis=-1, keepdims=True)
    cen = conv - mu
    var = jnp.mean(cen * cen, axis=-1, keepdims=True)
    ln = cen * lax.rsqrt(var + EPS) * lng_ref[...] + lnb_ref[...]
    y = (ln * jax.nn.sigmoid(ln)).astype(BF16)
    b = jnp.dot(y, cproj_ref[...], preferred_element_type=F32) + cbias_ref[...]

    mix = gates_ref[:, :d_model] * a + gates_ref[:, d_model:] * b
    o_ref[...] = x + jnp.dot(mix.astype(BF16), wout_ref[...], preferred_element_type=F32)


def _mixer(x, h, u, glu, w_gate, b_gate, w_group, pool_scale, pool_proj,
           dw_w, dw_b, ln_g, ln_b, conv_proj, conv_bias, w_out, *, seq_len, casts=(), tm=256, row_chunk=128):
    m, d = x.shape
    d_pool = u.shape[1]
    d_conv = glu.shape[1]
    conv_width = dw_w.shape[0]
    assert seq_len % tm == 0 and tm % HALO == 0 and conv_width // 2 < HALO
    hb = tm // HALO
    last_hb = m // HALO - 1

    def resident(shape):
        return pl.BlockSpec(shape, lambda i: (0,) * len(shape), pipeline_mode=pl.Buffered(1))

    def halo_specs(c):
        return [
            pl.BlockSpec((HALO, c), lambda i: (jnp.maximum(i * hb - 1, 0), 0)),
            pl.BlockSpec((tm, c), lambda i: (i, 0)),
            pl.BlockSpec((HALO, c), lambda i: (jnp.minimum((i + 1) * hb, last_hb), 0)),
        ]

    grid = (m // tm,)
    cast_in, cast_out, cast_shapes = _plan_casts(casts, grid)
    kern = functools.partial(_mixer_kernel, tm=tm, seq_len=seq_len, conv_width=conv_width,
                             row_chunk=row_chunk, n_casts=len(casts))
    out, *cast_results = pl.pallas_call(
        kern,
        out_shape=(jax.ShapeDtypeStruct((m, d), F32), *cast_shapes),
        grid=grid,
        in_specs=[pl.BlockSpec((tm, d), lambda i: (i, 0)), pl.BlockSpec((tm, d), lambda i: (i, 0))]
        + halo_specs(d_pool) + halo_specs(d_conv)
        + [resident(a.shape) for a in (w_gate, b_gate, w_group, pool_scale, pool_proj,
                                       dw_w, dw_b, ln_g, ln_b, conv_proj, conv_bias, w_out)]
        + list(cast_in),
        out_specs=(pl.BlockSpec((tm, d), lambda i: (i, 0)), *cast_out),
        scratch_shapes=[
            pltpu.VMEM((d_pool // LANES, tm + 2 * HALO, LANES), F32),
            pltpu.VMEM((d_conv // LANES, tm + 2 * HALO, LANES), F32),
            pltpu.VMEM((tm, d_conv), F32),
            pltpu.VMEM((tm, w_gate.shape[1]), F32),
        ],
        compiler_params=pltpu.CompilerParams(
            dimension_semantics=("parallel",),
            vmem_limit_bytes=VMEM_LIMIT),
        name="mixer",
    )(x, h, u, u, u, glu, glu, glu, w_gate, b_gate, w_group, pool_scale, pool_proj,
      dw_w, dw_b, ln_g, ln_b, conv_proj, conv_bias, w_out, *[job[0] for job in casts])
    return out, cast_results


def kernel(x, ffn1_norm, ffn1_w_gate, ffn1_w_up, ffn1_w_down, mix_norm, w_in, b_gate, pool_w_group, pool_scale, pool_w_proj, conv_dw_w, conv_dw_b, conv_ln_g, conv_ln_b, conv_w_proj, conv_b_proj, w_out, ffn2_norm, ffn2_w_gate, ffn2_w_up, ffn2_w_down, final_norm):
    bsz, seq_len, d = x.shape
    depth = ffn1_norm.shape[0]
    d_pool = pool_scale.shape[1]
    d_conv = conv_dw_b.shape[1]
    row = lambda v: v.reshape(1, -1)
    fn = row(final_norm)

    xf = x.reshape(bsz * seq_len, d)
    for l in range(depth):
        assert d_pool == d_conv
        n_uv = d_pool + 2 * d_conv
        whole = lambda w: (w, 0, w.shape[1])
        n_grp, grp = pool_w_group.shape[1:3]
        xf, (w_uv, w_g, wg2, wu2, w_o, w_pp, w_cp, w_grp) = _ffn(
            xf, row(ffn1_norm[l]), ffn1_w_gate[l].astype(BF16), ffn1_w_up[l].astype(BF16),
            ffn1_w_down[l].astype(BF16), fn, final_norm=False,
            casts=[(w_in[l], 0, n_uv), (w_in[l], n_uv, w_in.shape[2] - n_uv),
                   whole(ffn2_w_gate[l]), whole(ffn2_w_up[l]),
                   whole(w_out[l]), whole(pool_w_proj[l]), whole(conv_w_proj[l]),
                   whole(pool_w_group[l].reshape(n_grp * grp, grp))])
        u, glu, h = _in_proj(xf, row(mix_norm[l]), w_uv, c=d_pool)
        xf, (wd2,) = _mixer(xf, h, u, glu, w_g, row(b_gate[l]),
                            w_grp.reshape(n_grp, grp, grp), row(pool_scale[l]), w_pp,
                            conv_dw_w[l], row(conv_dw_b[l]), row(conv_ln_g[l]), row(conv_ln_b[l]),
                            w_cp, row(conv_b_proj[l]), w_o, seq_len=seq_len,
                            casts=[whole(ffn2_w_down[l])])
        xf, _ = _ffn(xf, row(ffn2_norm[l]), wg2, wu2, wd2, fn, final_norm=(l == depth - 1))
    return xf.reshape(bsz, seq_len, d)
```

```python
import functools

import jax
import jax.numpy as jnp
from jax import lax
from jax.experimental import pallas as pl
from jax.experimental.pallas import tpu as pltpu

EPS = 1e-6
POOL_WINDOWS = (2, 4, 8, 16)
HALO = 16
LANES = 128
BF16_ROWS = 16
FFN_ROWS = 1024
VMEM_LIMIT = 62 * 1024 * 1024

F32 = jnp.float32
BF16 = jnp.bfloat16


def _rmsnorm(x, g):
    return x * lax.rsqrt(jnp.mean(x * x, axis=-1, keepdims=True) + EPS) * g


def _cast_specs(job, grid):
    src, c0, cn = job
    gi, gf = grid if len(grid) == 2 else (grid[0], 1)
    rows = src.shape[0]
    br = rows // gi
    assert rows % gi == 0 and br % BF16_ROWS == 0
    nf = max(n for n in range(1, gf + 1)
             if cn % n == 0 and (cn // n) % LANES == 0 and c0 % (cn // n) == 0)
    bc = cn // nf
    off = c0 // bc
    col = lambda g: jnp.minimum(g[1], nf - 1) if len(g) == 2 else 0
    in_spec = pl.BlockSpec((br, bc), lambda *g: (g[0], off + col(g)))
    out_spec = pl.BlockSpec((br, bc), lambda *g: (g[0], col(g)))
    return in_spec, out_spec, jax.ShapeDtypeStruct((rows, cn), BF16)


def _plan_casts(casts, grid):
    return zip(*[_cast_specs(job, grid) for job in casts]) if casts else ((), (), ())


def _run_casts(src_refs, dst_refs):
    for src_ref, dst_ref in zip(src_refs, dst_refs):
        dst_ref[...] = src_ref[...].astype(BF16)


def _ffn_kernel(x_ref, g_ref, wg_ref, wu_ref, wd_ref, fn_ref, *rest, final_norm, n_casts, skip):
    cast_srcs, o_ref = rest[:n_casts], rest[n_casts]
    cast_dsts, w_copies, h_ref = rest[n_casts + 1:2 * n_casts + 1], rest[2 * n_casts + 1:-1], rest[-1]
    f = pl.program_id(1)
    active = pl.program_id(0) >= skip

    @pl.when((f == 0) & active)
    def _():
        x = x_ref[...]
        h_ref[...] = _rmsnorm(x, g_ref[...]).astype(BF16)
        o_ref[...] = x

    def step():
        _run_casts(cast_srcs, cast_dsts)
        wg, wu, wd = wg_ref[...], wu_ref[...], wd_ref[...]
        if w_copies:
            wg, wu, wd = wg.astype(BF16), wu.astype(BF16), wd.astype(BF16)
            for ref, w in zip(w_copies, (wg, wu, wd)):
                ref[...] = w
        h = h_ref[...]
        gate = jnp.dot(h, wg, preferred_element_type=F32)
        up = jnp.dot(h, wu, preferred_element_type=F32)
        act = (gate * jax.nn.sigmoid(gate)) * (up * 0.5)
        o_ref[...] += jnp.dot(act.astype(BF16), wd, preferred_element_type=F32)

    if skip:
        pl.when(active)(step)
        pl.when(jnp.logical_not(active))(lambda: _run_casts(cast_srcs, cast_dsts))
    else:
        step()

    if final_norm:
        @pl.when((f == pl.num_programs(1) - 1) & active)
        def _():
            o_ref[...] = _rmsnorm(o_ref[...], fn_ref[...])


def _ffn(x, g, wg, wu, wd, fn, *, final_norm, casts=(), rows=None, skip=0, tm=1024, tf=512, name="ffn"):
    d = x.shape[1]
    dff = wg.shape[1]
    n_blocks = (rows or x.shape[0]) // tm
    count = n_blocks - skip
    grid = (n_blocks, dff // tf)
    cast_in, cast_out, cast_shapes = _plan_casts(casts, grid)
    fcol = lambda i, f: jnp.where(i < skip, 0, f) if skip else f
    w_specs = [pl.BlockSpec((d, tf), lambda i, f: (0, fcol(i, f))),
               pl.BlockSpec((d, tf), lambda i, f: (0, fcol(i, f))),
               pl.BlockSpec((tf, d), lambda i, f: (fcol(i, f), 0))]
    copy_weights = wg.dtype == F32
    assert not copy_weights or n_blocks == 1
    w_out_specs = w_specs if copy_weights else []
    w_out_shapes = [jax.ShapeDtypeStruct(w.shape, BF16) for w in (wg, wu, wd)] if copy_weights else []
    out, *results = pl.pallas_call(
        functools.partial(_ffn_kernel, final_norm=final_norm, n_casts=len(casts), skip=skip),
        out_shape=(jax.ShapeDtypeStruct((count * tm, d), F32), *cast_shapes, *w_out_shapes),
        grid=grid,
        in_specs=[
            pl.BlockSpec((tm, d), lambda i, f: (jnp.maximum(i, skip), 0)),
            pl.BlockSpec((1, d), lambda i, f: (0, 0)),
            *w_specs,
            pl.BlockSpec((1, d), lambda i, f: (0, 0)),
            *cast_in,
        ],
        out_specs=(pl.BlockSpec((tm, d), lambda i, f: (jnp.maximum(i - skip, 0), 0)), *cast_out, *w_out_specs),
        scratch_shapes=[pltpu.VMEM((tm, d), BF16)],
        compiler_params=pltpu.CompilerParams(
            dimension_semantics=("parallel", "arbitrary"),
            vmem_limit_bytes=VMEM_LIMIT),
        name=name,
    )(x, g, wg, wu, wd, fn, *[job[0] for job in casts])
    return out, results


def _in_proj_kernel(xa_ref, xb_ref, g_ref, wu_ref, wv1_ref, wv2_ref, u_ref, glu_ref, h_ref, *, blocks_a):
    i = pl.program_id(0)

    @pl.when((pl.program_id(1) == 0) & (i < blocks_a))
    def _():
        h_ref[...] = _rmsnorm(xa_ref[...], g_ref[...]).astype(BF16)

    @pl.when((pl.program_id(1) == 0) & (i >= blocks_a))
    def _():
        h_ref[...] = _rmsnorm(xb_ref[...], g_ref[...]).astype(BF16)

    h = h_ref[...]
    u_ref[...] = jnp.dot(h, wu_ref[...], preferred_element_type=F32)
    v1 = jnp.dot(h, wv1_ref[...], preferred_element_type=F32)
    v2 = jnp.dot(h, wv2_ref[...], preferred_element_type=F32)
    glu_ref[...] = v1 * jax.nn.sigmoid(v2)


def _in_proj(xa, xb, g, w_uv, *, c, tm=1024, tn=512):
    d = xa.shape[1]
    m = xa.shape[0] + xb.shape[0]
    blocks_a, blocks_b = xa.shape[0] // tm, xb.shape[0] // tm
    assert w_uv.shape[1] == 3 * c and c % tn == 0 and blocks_a * tm == xa.shape[0]
    nb = c // tn
    o_spec = pl.BlockSpec((tm, tn), lambda i, j: (i, j))
    return pl.pallas_call(
        functools.partial(_in_proj_kernel, blocks_a=blocks_a),
        out_shape=(jax.ShapeDtypeStruct((m, c), F32), jax.ShapeDtypeStruct((m, c), F32),
                   jax.ShapeDtypeStruct((m, d), BF16)),
        grid=(m // tm, nb),
        in_specs=[
            pl.BlockSpec((tm, d), lambda i, j: (jnp.minimum(i, blocks_a - 1), 0),
                         pipeline_mode=pl.Buffered(1)),
            pl.BlockSpec((tm, d), lambda i, j: (jnp.clip(i - blocks_a, 0, blocks_b - 1), 0)),
            pl.BlockSpec((1, d), lambda i, j: (0, 0)),
            pl.BlockSpec((d, tn), lambda i, j: (0, j)),
            pl.BlockSpec((d, tn), lambda i, j: (0, nb + j)),
            pl.BlockSpec((d, tn), lambda i, j: (0, 2 * nb + j)),
        ],
        out_specs=(o_spec, o_spec, pl.BlockSpec((tm, d), lambda i, j: (i, 0))),
        compiler_params=pltpu.CompilerParams(
            dimension_semantics=("parallel", "arbitrary"),
            vmem_limit_bytes=VMEM_LIMIT),
        name="in_proj",
    )(xa, xb, g, w_uv, w_uv, w_uv)


def _zero_bits(v):
    bits = lax.bitcast_convert_type(v, jnp.uint32)
    return ((bits >> 16) >> 16)[0:1, :]


def _tied(w, zero_bits):
    if zero_bits is None:
        return w
    return lax.bitcast_convert_type(lax.bitcast_convert_type(w, jnp.uint32) | zero_bits, F32)


def _fill_halo_buffer(buf_ref, prev_ref, main_ref, next_ref, at_seq_start, at_seq_end, tm):
    for c in range(buf_ref.shape[0]):
        lanes = slice(c * LANES, (c + 1) * LANES)
        prev = prev_ref[:, lanes]
        nxt = next_ref[:, lanes]
        buf_ref[c, 0:HALO, :] = jnp.where(at_seq_start, jnp.zeros_like(prev), prev)
        buf_ref[c, HALO:HALO + tm, :] = main_ref[:, lanes]
        buf_ref[c, HALO + tm:HALO + tm + HALO, :] = jnp.where(at_seq_end, jnp.zeros_like(nxt), nxt)


def _mixer_kernel(xa_ref, xb_ref, h_ref, up_ref, um_ref, un_ref, gp_ref, gm_ref, gn_ref,
                  wgate_ref, bgate_ref, wgrp_ref, pscale_ref, pproj_ref,
                  dww_ref, dwb_ref, lng_ref, lnb_ref, cproj_ref, cbias_ref, wout_ref, *rest,
                  tm, seq_len, conv_width, row_chunk, n_casts, blocks_a):
    cast_srcs, o_ref = rest[:n_casts], rest[n_casts]
    cast_dsts = rest[n_casts + 1:2 * n_casts + 1]
    ubuf_ref, gbuf_ref, conv_ref, gates_ref = rest[2 * n_casts + 1:]
    i = pl.program_id(0)
    blocks_per_seq = seq_len // tm
    seq_block = i % blocks_per_seq
    at_seq_start = seq_block == 0
    at_seq_end = seq_block == blocks_per_seq - 1
    d_conv = gm_ref.shape[1]
    d_model = xa_ref.shape[1]
    group_chunks = um_ref.shape[1] // len(POOL_WINDOWS) // LANES

    _run_casts(cast_srcs, cast_dsts)
    _fill_halo_buffer(ubuf_ref, up_ref, um_ref, un_ref, at_seq_start, at_seq_end, tm)
    _fill_halo_buffer(gbuf_ref, gp_ref, gm_ref, gn_ref, at_seq_start, at_seq_end, tm)

    first = HALO - conv_width // 2
    n_chunks = d_conv // LANES
    gate_cols = gates_ref.shape[1] // n_chunks
    tie = None
    for c in range(n_chunks):
        lanes = slice(c * LANES, (c + 1) * LANES)
        w_rows = [jnp.broadcast_to(_tied(dww_ref[k:k + 1, lanes], tie), (row_chunk, LANES))
                  for k in range(conv_width)]
        bias = dwb_ref[:, lanes]
        for t0 in range(0, tm, row_chunk):
            acc = gbuf_ref[c, t0 + first:t0 + first + row_chunk, :] * w_rows[0]
            for k in range(1, conv_width):
                acc = acc + gbuf_ref[c, t0 + first + k:t0 + first + k + row_chunk, :] * w_rows[k]
            conv_ref[t0:t0 + row_chunk, lanes] = acc + bias

        gc = slice(c * gate_cols, (c + 1) * gate_cols)
        logits = jnp.dot(h_ref[...], wgate_ref[:, gc], preferred_element_type=F32) + bgate_ref[:, gc]
        sig = jax.nn.sigmoid(logits)
        gates_ref[:, gc] = sig
        tie = _zero_bits(sig[tm - 8:tm, gate_cols - LANES:gate_cols])

    pos = seq_block * tm + lax.broadcasted_iota(jnp.int32, (tm, 1), 0)
    mixed = []
    for gi, w in enumerate(POOL_WINDOWS):
        lo = jnp.maximum(pos - w // 2, 0)
        hi = jnp.minimum(pos + (w - w // 2), seq_len)
        inv_count = 1.0 / (hi - lo).astype(F32)
        pooled = []
        for c in range(gi * group_chunks, (gi + 1) * group_chunks):
            s = ubuf_ref[c, HALO - w // 2:HALO - w // 2 + tm, :]
            for dlt in range(-(w // 2) + 1, w - w // 2):
                s = s + ubuf_ref[c, HALO + dlt:HALO + dlt + tm, :]
            pooled.append((s * inv_count - ubuf_ref[c, HALO:HALO + tm, :]).astype(BF16))
        mg = jnp.dot(jnp.concatenate(pooled, axis=1), wgrp_ref[gi], preferred_element_type=F32)
        cols = slice(gi * group_chunks * LANES, (gi + 1) * group_chunks * LANES)
        mixed.append((mg * pscale_ref[:, cols]).astype(BF16))
    mixed = jnp.concatenate(mixed, axis=1)
    a = jnp.dot(mixed, pproj_ref[...], preferred_element_type=F32)

    conv = conv_ref[...]
    mu = jnp.mean(conv, axis=-1, keepdims=True)
    cen = conv - mu
    var = jnp.mean(cen * cen, axis=-1, keepdims=True)
    ln = cen * lax.rsqrt(var + EPS) * lng_ref[...] + lnb_ref[...]
    y = (ln * jax.nn.sigmoid(ln)).astype(BF16)
    b = jnp.dot(y, cproj_ref[...], preferred_element_type=F32) + cbias_ref[...]

    mix = gates_ref[:, :d_model] * a + gates_ref[:, d_model:] * b
    x = jnp.where(i < blocks_a, xa_ref[...], xb_ref[...])
    o_ref[...] = x + jnp.dot(mix.astype(BF16), wout_ref[...], preferred_element_type=F32)


def _mixer(xa, xb, h, u, glu, w_gate, b_gate, w_group, pool_scale, pool_proj,
           dw_w, dw_b, ln_g, ln_b, conv_proj, conv_bias, w_out, *, seq_len, casts=(), tm=256, row_chunk=128):
    d = xa.shape[1]
    m = xa.shape[0] + xb.shape[0]
    blocks_a, blocks_b = xa.shape[0] // tm, xb.shape[0] // tm
    d_pool = u.shape[1]
    d_conv = glu.shape[1]
    conv_width = dw_w.shape[0]
    assert seq_len % tm == 0 and tm % HALO == 0 and conv_width // 2 < HALO
    hb = tm // HALO
    last_hb = m // HALO - 1

    def resident(shape):
        return pl.BlockSpec(shape, lambda i: (0,) * len(shape), pipeline_mode=pl.Buffered(1))

    def halo_specs(c):
        return [
            pl.BlockSpec((HALO, c), lambda i: (jnp.maximum(i * hb - 1, 0), 0)),
            pl.BlockSpec((tm, c), lambda i: (i, 0)),
            pl.BlockSpec((HALO, c), lambda i: (jnp.minimum((i + 1) * hb, last_hb), 0)),
        ]

    grid = (m // tm,)
    cast_in, cast_out, cast_shapes = _plan_casts(casts, grid)
    kern = functools.partial(_mixer_kernel, tm=tm, seq_len=seq_len, conv_width=conv_width,
                             row_chunk=row_chunk, n_casts=len(casts), blocks_a=blocks_a)
    out, *cast_results = pl.pallas_call(
        kern,
        out_shape=(jax.ShapeDtypeStruct((m, d), F32), *cast_shapes),
        grid=grid,
        in_specs=[pl.BlockSpec((tm, d), lambda i: (jnp.minimum(i, blocks_a - 1), 0),
                               pipeline_mode=pl.Buffered(1)),
                  pl.BlockSpec((tm, d), lambda i: (jnp.clip(i - blocks_a, 0, blocks_b - 1), 0)),
                  pl.BlockSpec((tm, d), lambda i: (i, 0))]
        + halo_specs(d_pool) + halo_specs(d_conv)
        + [resident(a.shape) for a in (w_gate, b_gate, w_group, pool_scale, pool_proj,
                                       dw_w, dw_b, ln_g, ln_b, conv_proj, conv_bias, w_out)]
        + list(cast_in),
        out_specs=(pl.BlockSpec((tm, d), lambda i: (i, 0)), *cast_out),
        scratch_shapes=[
            pltpu.VMEM((d_pool // LANES, tm + 2 * HALO, LANES), F32),
            pltpu.VMEM((d_conv // LANES, tm + 2 * HALO, LANES), F32),
            pltpu.VMEM((tm, d_conv), F32),
            pltpu.VMEM((tm, w_gate.shape[1]), F32),
        ],
        compiler_params=pltpu.CompilerParams(
            dimension_semantics=("parallel",),
            vmem_limit_bytes=VMEM_LIMIT),
        name="mixer",
    )(xa, xb, h, u, u, u, glu, glu, glu, w_gate, b_gate, w_group, pool_scale, pool_proj,
      dw_w, dw_b, ln_g, ln_b, conv_proj, conv_bias, w_out, *[job[0] for job in casts])
    return out, cast_results


def kernel(x, ffn1_norm, ffn1_w_gate, ffn1_w_up, ffn1_w_down, mix_norm, w_in, b_gate, pool_w_group, pool_scale, pool_w_proj, conv_dw_w, conv_dw_b, conv_ln_g, conv_ln_b, conv_w_proj, conv_b_proj, w_out, ffn2_norm, ffn2_w_gate, ffn2_w_up, ffn2_w_down, final_norm):
    bsz, seq_len, d = x.shape
    depth = ffn1_norm.shape[0]
    d_pool = pool_scale.shape[1]
    d_conv = conv_dw_b.shape[1]
    row = lambda v: v.reshape(1, -1)
    fn = row(final_norm)

    xf = x.reshape(bsz * seq_len, d)
    for l in range(depth):
        assert d_pool == d_conv
        n_uv = d_pool + 2 * d_conv
        whole = lambda w: (w, 0, w.shape[1])
        n_grp, grp = pool_w_group.shape[1:3]
        xa, (wg1, wu1, wd1) = _ffn(xf, row(ffn1_norm[l]), ffn1_w_gate[l], ffn1_w_up[l], ffn1_w_down[l], fn,
                                   final_norm=False, rows=FFN_ROWS, tm=FFN_ROWS, tf=256, name="ffn_head")
        xb, (w_uv, w_g, wg2, wu2, w_o, w_pp, w_cp, w_grp) = _ffn(
            xf, row(ffn1_norm[l]), wg1, wu1, wd1, fn, final_norm=False, skip=1, tm=FFN_ROWS,
            casts=[(w_in[l], 0, n_uv), (w_in[l], n_uv, w_in.shape[2] - n_uv),
                   whole(ffn2_w_gate[l]), whole(ffn2_w_up[l]),
                   whole(w_out[l]), whole(pool_w_proj[l]), whole(conv_w_proj[l]),
                   whole(pool_w_group[l].reshape(n_grp * grp, grp))])
        u, glu, h = _in_proj(xa, xb, row(mix_norm[l]), w_uv, c=d_pool)
        xf, (wd2,) = _mixer(xa, xb, h, u, glu, w_g, row(b_gate[l]),
                            w_grp.reshape(n_grp, grp, grp), row(pool_scale[l]), w_pp,
                            conv_dw_w[l], row(conv_dw_b[l]), row(conv_ln_g[l]), row(conv_ln_b[l]),
                            w_cp, row(conv_b_proj[l]), w_o, seq_len=seq_len,
                            casts=[whole(ffn2_w_down[l])])
        xf, _ = _ffn(xf, row(ffn2_norm[l]), wg2, wu2, wd2, fn, final_norm=(l == depth - 1),
                     tm=FFN_ROWS, name="ffn_final" if l == depth - 1 else "ffn_second")
    return xf.reshape(bsz, seq_len, d)
```

```python
import functools
import math

import jax
import jax.numpy as jnp
from jax import lax
from jax.experimental import pallas as pl
from jax.experimental.pallas import tpu as pltpu

EPS = 1e-6
POOL_WINDOWS = (2, 4, 8, 16)
HALO = 16
LANES = 128
BF16_ROWS = 16
FFN_ROWS = 1024
VMEM_LIMIT = 63 * 1024 * 1024

F32 = jnp.float32
BF16 = jnp.bfloat16


def _rmsnorm(x, g):
    return x * lax.rsqrt(jnp.mean(x * x, axis=-1, keepdims=True) + EPS) * g


def _cast_specs(job, grid):
    src, c0, cn = job
    steps = math.prod(grid)
    rows = src.shape[0]
    plans = [(nr, nc) for nr in range(1, rows // BF16_ROWS + 1) for nc in range(1, cn // LANES + 1)
             if rows % (nr * BF16_ROWS) == 0 and cn % (nc * LANES) == 0 and c0 % (cn // nc) == 0
             and nr * nc <= steps]
    nr, nc = max(plans, key=lambda p: (p[0] * p[1], min(rows // p[0], cn // p[1])))
    br, bc = rows // nr, cn // nc
    off = c0 // bc

    def block(g):
        step = g[0] * grid[1] + g[1] if len(g) == 2 else g[0]
        return jnp.minimum(step, nr * nc - 1)

    in_spec = pl.BlockSpec((br, bc), lambda *g: (block(g) // nc, off + block(g) % nc))
    out_spec = pl.BlockSpec((br, bc), lambda *g: (block(g) // nc, block(g) % nc))
    return in_spec, out_spec, jax.ShapeDtypeStruct((rows, cn), BF16)


def _plan_casts(casts, grid):
    return zip(*[_cast_specs(job, grid) for job in casts]) if casts else ((), (), ())


def _run_casts(src_refs, dst_refs):
    for src_ref, dst_ref in zip(src_refs, dst_refs):
        dst_ref[...] = src_ref[...].astype(BF16)


def _ffn_kernel(x_ref, g_ref, wg_ref, wu_ref, wd_ref, fn_ref, *rest, final_norm, n_casts):
    cast_srcs, o_ref = rest[:n_casts], rest[n_casts]
    cast_dsts, w_copies, h_ref = rest[n_casts + 1:2 * n_casts + 1], rest[2 * n_casts + 1:-1], rest[-1]
    f = pl.program_id(1)

    @pl.when(f == 0)
    def _():
        x = x_ref[...]
        h_ref[...] = _rmsnorm(x, g_ref[...]).astype(BF16)
        o_ref[...] = x

    _run_casts(cast_srcs, cast_dsts)
    wg, wu, wd = wg_ref[...], wu_ref[...], wd_ref[...]
    if w_copies:
        wg, wu, wd = wg.astype(BF16), wu.astype(BF16), wd.astype(BF16)
        for ref, w in zip(w_copies, (wg, wu, wd)):
            ref[...] = w
    h = h_ref[...]
    gate = jnp.dot(h, wg, preferred_element_type=F32)
    up = jnp.dot(h, wu, preferred_element_type=F32)
    act = (gate * jax.nn.sigmoid(gate)) * (up * 0.5)
    o_ref[...] += jnp.dot(act.astype(BF16), wd, preferred_element_type=F32)

    if final_norm:
        @pl.when(f == pl.num_programs(1) - 1)
        def _():
            o_ref[...] = _rmsnorm(o_ref[...], fn_ref[...])


def _ffn(x, g, wg, wu, wd, fn, *, final_norm, casts=(), row_blocks=None, tm=1024, tf=512, name="ffn"):
    d = x.shape[1]
    dff = wg.shape[1]
    first, count = row_blocks or (0, x.shape[0] // tm)
    grid = (count, dff // tf)
    cast_in, cast_out, cast_shapes = _plan_casts(casts, grid)
    w_specs = [pl.BlockSpec((d, tf), lambda i, f: (0, f)),
               pl.BlockSpec((d, tf), lambda i, f: (0, f)),
               pl.BlockSpec((tf, d), lambda i, f: (f, 0))]
    copy_weights = wg.dtype == F32
    assert not copy_weights or count == 1
    w_out_specs = w_specs if copy_weights else []
    w_out_shapes = [jax.ShapeDtypeStruct(w.shape, BF16) for w in (wg, wu, wd)] if copy_weights else []
    out, *results = pl.pallas_call(
        functools.partial(_ffn_kernel, final_norm=final_norm, n_casts=len(casts)),
        out_shape=(jax.ShapeDtypeStruct((count * tm, d), F32), *cast_shapes, *w_out_shapes),
        grid=grid,
        in_specs=[
            pl.BlockSpec((tm, d), lambda i, f: (first + i, 0)),
            pl.BlockSpec((1, d), lambda i, f: (0, 0)),
            *w_specs,
            pl.BlockSpec((1, d), lambda i, f: (0, 0)),
            *cast_in,
        ],
        out_specs=(pl.BlockSpec((tm, d), lambda i, f: (i, 0)), *cast_out, *w_out_specs),
        scratch_shapes=[pltpu.VMEM((tm, d), BF16)],
        compiler_params=pltpu.CompilerParams(
            dimension_semantics=("parallel", "arbitrary"),
            vmem_limit_bytes=VMEM_LIMIT),
        name=name,
    )(x, g, wg, wu, wd, fn, *[job[0] for job in casts])
    return out, results


def _in_proj_kernel(xa_ref, xb_ref, g_ref, wu_ref, wv1_ref, wv2_ref, u_ref, glu_ref, h_ref, *, blocks_a):
    i = pl.program_id(0)

    @pl.when((pl.program_id(1) == 0) & (i < blocks_a))
    def _():
        h_ref[...] = _rmsnorm(xa_ref[...], g_ref[...]).astype(BF16)

    @pl.when((pl.program_id(1) == 0) & (i >= blocks_a))
    def _():
        h_ref[...] = _rmsnorm(xb_ref[...], g_ref[...]).astype(BF16)

    h = h_ref[...]
    u_ref[...] = jnp.dot(h, wu_ref[...], preferred_element_type=F32)
    v1 = jnp.dot(h, wv1_ref[...], preferred_element_type=F32)
    v2 = jnp.dot(h, wv2_ref[...], preferred_element_type=F32)
    glu_ref[...] = v1 * jax.nn.sigmoid(v2)


def _in_proj(xa, xb, g, w_uv, *, c, tm=512, tn=1024):
    d = xa.shape[1]
    m = xa.shape[0] + xb.shape[0]
    blocks_a, blocks_b = xa.shape[0] // tm, xb.shape[0] // tm
    assert w_uv.shape[1] == 3 * c and c % tn == 0 and blocks_a * tm == xa.shape[0]
    nb = c // tn
    o_spec = pl.BlockSpec((tm, tn), lambda i, j: (i, j))
    return pl.pallas_call(
        functools.partial(_in_proj_kernel, blocks_a=blocks_a),
        out_shape=(jax.ShapeDtypeStruct((m, c), F32), jax.ShapeDtypeStruct((m, c), F32),
                   jax.ShapeDtypeStruct((m, d), BF16)),
        grid=(m // tm, nb),
        in_specs=[
            pl.BlockSpec((tm, d), lambda i, j: (jnp.minimum(i, blocks_a - 1), 0),
                         pipeline_mode=pl.Buffered(1)),
            pl.BlockSpec((tm, d), lambda i, j: (jnp.clip(i - blocks_a, 0, blocks_b - 1), 0)),
            pl.BlockSpec((1, d), lambda i, j: (0, 0)),
            pl.BlockSpec((d, tn), lambda i, j: (0, j)),
            pl.BlockSpec((d, tn), lambda i, j: (0, nb + j)),
            pl.BlockSpec((d, tn), lambda i, j: (0, 2 * nb + j)),
        ],
        out_specs=(o_spec, o_spec, pl.BlockSpec((tm, d), lambda i, j: (i, 0))),
        compiler_params=pltpu.CompilerParams(
            dimension_semantics=("parallel", "arbitrary"),
            vmem_limit_bytes=VMEM_LIMIT),
        name="in_proj",
    )(xa, xb, g, w_uv, w_uv, w_uv)


def _zero_bits(v):
    bits = lax.bitcast_convert_type(v, jnp.uint32)
    return ((bits >> 16) >> 16)[0:1, :]


def _tied(w, zero_bits):
    if zero_bits is None:
        return w
    return lax.bitcast_convert_type(lax.bitcast_convert_type(w, jnp.uint32) | zero_bits, F32)


def _fill_halo_buffer(buf_ref, prev_ref, main_ref, next_ref, at_seq_start, at_seq_end, tm):
    for c in range(buf_ref.shape[0]):
        lanes = slice(c * LANES, (c + 1) * LANES)
        prev = prev_ref[:, lanes]
        nxt = next_ref[:, lanes]
        buf_ref[c, 0:HALO, :] = jnp.where(at_seq_start, jnp.zeros_like(prev), prev)
        buf_ref[c, HALO:HALO + tm, :] = main_ref[:, lanes]
        buf_ref[c, HALO + tm:HALO + tm + HALO, :] = jnp.where(at_seq_end, jnp.zeros_like(nxt), nxt)


def _mixer_kernel(xa_ref, xb_ref, h_ref, up_ref, um_ref, un_ref, gp_ref, gm_ref, gn_ref,
                  wgate_ref, bgate_ref, wgrp_ref, pscale_ref, pproj_ref,
                  dww_ref, dwb_ref, lng_ref, lnb_ref, cproj_ref, cbias_ref, wout_ref, *rest,
                  tm, seq_len, conv_width, row_chunk, n_casts, blocks_a):
    cast_srcs, o_ref = rest[:n_casts], rest[n_casts]
    cast_dsts = rest[n_casts + 1:2 * n_casts + 1]
    ubuf_ref, gbuf_ref, conv_ref, gates_ref = rest[2 * n_casts + 1:]
    i = pl.program_id(0)
    blocks_per_seq = seq_len // tm
    seq_block = i % blocks_per_seq
    at_seq_start = seq_block == 0
    at_seq_end = seq_block == blocks_per_seq - 1
    d_conv = gm_ref.shape[1]
    d_model = xa_ref.shape[1]
    group_chunks = um_ref.shape[1] // len(POOL_WINDOWS) // LANES

    _run_casts(cast_srcs, cast_dsts)
    _fill_halo_buffer(ubuf_ref, up_ref, um_ref, un_ref, at_seq_start, at_seq_end, tm)
    _fill_halo_buffer(gbuf_ref, gp_ref, gm_ref, gn_ref, at_seq_start, at_seq_end, tm)

    first = HALO - conv_width // 2
    n_chunks = d_conv // LANES
    gate_cols = gates_ref.shape[1] // n_chunks
    tie = None
    for c in range(n_chunks):
        lanes = slice(c * LANES, (c + 1) * LANES)
        w_rows = [jnp.broadcast_to(_tied(dww_ref[k:k + 1, lanes], tie), (row_chunk, LANES))
                  for k in range(conv_width)]
        bias = dwb_ref[:, lanes]
        for t0 in range(0, tm, row_chunk):
            acc = gbuf_ref[c, t0 + first:t0 + first + row_chunk, :] * w_rows[0]
            for k in range(1, conv_width):
                acc = acc + gbuf_ref[c, t0 + first + k:t0 + first + k + row_chunk, :] * w_rows[k]
            conv_ref[t0:t0 + row_chunk, lanes] = acc + bias

        gc = slice(c * gate_cols, (c + 1) * gate_cols)
        logits = jnp.dot(h_ref[...], wgate_ref[:, gc], preferred_element_type=F32) + bgate_ref[:, gc]
        sig = jax.nn.sigmoid(logits)
        gates_ref[:, gc] = sig
        tie = _zero_bits(sig[tm - 8:tm, gate_cols - LANES:gate_cols])

    pos = seq_block * tm + lax.broadcasted_iota(jnp.int32, (tm, 1), 0)
    mixed = []
    for gi, w in enumerate(POOL_WINDOWS):
        lo = jnp.maximum(pos - w // 2, 0)
        hi = jnp.minimum(pos + (w - w // 2), seq_len)
        inv_count = 1.0 / (hi - lo).astype(F32)
        pooled = []
        for c in range(gi * group_chunks, (gi + 1) * group_chunks):
            s = ubuf_ref[c, HALO - w // 2:HALO - w // 2 + tm, :]
            for dlt in range(-(w // 2) + 1, w - w // 2):
                s = s + ubuf_ref[c, HALO + dlt:HALO + dlt + tm, :]
            pooled.append((s * inv_count - ubuf_ref[c, HALO:HALO + tm, :]).astype(BF16))
        mg = jnp.dot(jnp.concatenate(pooled, axis=1), wgrp_ref[gi], preferred_element_type=F32)
        cols = slice(gi * group_chunks * LANES, (gi + 1) * group_chunks * LANES)
        mixed.append((mg * pscale_ref[:, cols]).astype(BF16))
    mixed = jnp.concatenate(mixed, axis=1)
    a = jnp.dot(mixed, pproj_ref[...], preferred_element_type=F32)

    conv = conv_ref[...]
    mu = jnp.mean(conv, axis=-1, keepdims=True)
    cen = conv - mu
    var = jnp.mean(cen * cen, axis=-1, keepdims=True)
    ln = cen * lax.rsqrt(var + EPS) * lng_ref[...] + lnb_ref[...]
    y = (ln * jax.nn.sigmoid(ln)).astype(BF16)
    b = jnp.dot(y, cproj_ref[...], preferred_element_type=F32) + cbias_ref[...]

    mix = gates_ref[:, :d_model] * a + gates_ref[:, d_model:] * b
    x = jnp.where(i < blocks_a, xa_ref[...], xb_ref[...])
    o_ref[...] = x + jnp.dot(mix.astype(BF16), wout_ref[...], preferred_element_type=F32)


def _mixer(xa, xb, h, u, glu, w_gate, b_gate, w_group, pool_scale, pool_proj,
           dw_w, dw_b, ln_g, ln_b, conv_proj, conv_bias, w_out, *, seq_len, casts=(), tm=256, row_chunk=128):
    d = xa.shape[1]
    m = xa.shape[0] + xb.shape[0]
    blocks_a, blocks_b = xa.shape[0] // tm, xb.shape[0] // tm
    d_pool = u.shape[1]
    d_conv = glu.shape[1]
    conv_width = dw_w.shape[0]
    assert seq_len % tm == 0 and tm % HALO == 0 and conv_width // 2 < HALO
    hb = tm // HALO
    last_hb = m // HALO - 1

    def resident(shape):
        return pl.BlockSpec(shape, lambda i: (0,) * len(shape), pipeline_mode=pl.Buffered(1))

    def halo_specs(c):
        return [
            pl.BlockSpec((HALO, c), lambda i: (jnp.maximum(i * hb - 1, 0), 0)),
            pl.BlockSpec((tm, c), lambda i: (i, 0)),
            pl.BlockSpec((HALO, c), lambda i: (jnp.minimum((i + 1) * hb, last_hb), 0)),
        ]

    grid = (m // tm,)
    cast_in, cast_out, cast_shapes = _plan_casts(casts, grid)
    kern = functools.partial(_mixer_kernel, tm=tm, seq_len=seq_len, conv_width=conv_width,
                             row_chunk=row_chunk, n_casts=len(casts), blocks_a=blocks_a)
    out, *cast_results = pl.pallas_call(
        kern,
        out_shape=(jax.ShapeDtypeStruct((m, d), F32), *cast_shapes),
        grid=grid,
        in_specs=[pl.BlockSpec((tm, d), lambda i: (jnp.minimum(i, blocks_a - 1), 0),
                               pipeline_mode=pl.Buffered(1)),
                  pl.BlockSpec((tm, d), lambda i: (jnp.clip(i - blocks_a, 0, blocks_b - 1), 0)),
                  pl.BlockSpec((tm, d), lambda i: (i, 0))]
        + halo_specs(d_pool) + halo_specs(d_conv)
        + [resident(a.shape) for a in (w_gate, b_gate, w_group, pool_scale, pool_proj,
                                       dw_w, dw_b, ln_g, ln_b, conv_proj, conv_bias, w_out)]
        + list(cast_in),
        out_specs=(pl.BlockSpec((tm, d), lambda i: (i, 0)), *cast_out),
        scratch_shapes=[
            pltpu.VMEM((d_pool // LANES, tm + 2 * HALO, LANES), F32),
            pltpu.VMEM((d_conv // LANES, tm + 2 * HALO, LANES), F32),
            pltpu.VMEM((tm, d_conv), F32),
            pltpu.VMEM((tm, w_gate.shape[1]), F32),
        ],
        compiler_params=pltpu.CompilerParams(
            dimension_semantics=("parallel",),
            vmem_limit_bytes=VMEM_LIMIT),
        name="mixer",
    )(xa, xb, h, u, u, u, glu, glu, glu, w_gate, b_gate, w_group, pool_scale, pool_proj,
      dw_w, dw_b, ln_g, ln_b, conv_proj, conv_bias, w_out, *[job[0] for job in casts])
    return out, cast_results


def kernel(x, ffn1_norm, ffn1_w_gate, ffn1_w_up, ffn1_w_down, mix_norm, w_in, b_gate, pool_w_group, pool_scale, pool_w_proj, conv_dw_w, conv_dw_b, conv_ln_g, conv_ln_b, conv_w_proj, conv_b_proj, w_out, ffn2_norm, ffn2_w_gate, ffn2_w_up, ffn2_w_down, final_norm):
    bsz, seq_len, d = x.shape
    depth = ffn1_norm.shape[0]
    d_pool = pool_scale.shape[1]
    d_conv = conv_dw_b.shape[1]
    row = lambda v: v.reshape(1, -1)
    fn = row(final_norm)

    xf = x.reshape(bsz * seq_len, d)
    for l in range(depth):
        assert d_pool == d_conv
        n_uv = d_pool + 2 * d_conv
        whole = lambda w: (w, 0, w.shape[1])
        n_grp, grp = pool_w_group.shape[1:3]
        n_blocks = xf.shape[0] // FFN_ROWS
        xa, (wg1, wu1, wd1) = _ffn(xf, row(ffn1_norm[l]), ffn1_w_gate[l], ffn1_w_up[l], ffn1_w_down[l], fn,
                                   final_norm=False, row_blocks=(0, 1), tm=FFN_ROWS, tf=256, name="ffn_head")
        xb, (w_uv, w_g, wg2, wu2, w_o, w_pp, w_cp, w_grp) = _ffn(
            xf, row(ffn1_norm[l]), wg1, wu1, wd1, fn, final_norm=False,
            row_blocks=(1, n_blocks - 1), tm=FFN_ROWS,
            casts=[(w_in[l], 0, n_uv), (w_in[l], n_uv, w_in.shape[2] - n_uv),
                   whole(ffn2_w_gate[l]), whole(ffn2_w_up[l]),
                   whole(w_out[l]), whole(pool_w_proj[l]), whole(conv_w_proj[l]),
                   whole(pool_w_group[l].reshape(n_grp * grp, grp))])
        u, glu, h = _in_proj(xa, xb, row(mix_norm[l]), w_uv, c=d_pool)
        xf, (wd2,) = _mixer(xa, xb, h, u, glu, w_g, row(b_gate[l]),
                            w_grp.reshape(n_grp, grp, grp), row(pool_scale[l]), w_pp,
                            conv_dw_w[l], row(conv_dw_b[l]), row(conv_ln_g[l]), row(conv_ln_b[l]),
                            w_cp, row(conv_b_proj[l]), w_o, seq_len=seq_len,
                            casts=[whole(ffn2_w_down[l])])
        xf, _ = _ffn(xf, row(ffn2_norm[l]), wg2, wu2, wd2, fn, final_norm=(l == depth - 1),
                     tm=FFN_ROWS, name="ffn_final" if l == depth - 1 else "ffn_second")
    return xf.reshape(bsz, seq_len, d)
```
